```python
import math
import jax, jax.numpy as jnp
from jax import lax
import numpy as np

D_MODEL = 1024
BATCH = 8
SEQ = 2048
DEPTH = 4

N_A_LAYERS = DEPTH // 2
N_B_LAYERS = DEPTH - N_A_LAYERS
S5_GROUP = 16
S5_GROUPS = D_MODEL // S5_GROUP
S5_STATE = 64
DT_MIN = 1e-3
DT_MAX = 1e-1
HEAD_DIM = 128
N_HEADS = D_MODEL // HEAD_DIM
MOBA_BLOCK = 256
MOBA_TOPK = 3
QUERY_CHUNK = 16
D_FF = 128 * ((8 * D_MODEL // 3 + 127) // 128)
CONV_WIDTH = 3
ROPE_THETA = 10000.0
EPS = 1e-6

kernel_name = "yoco_s5_moba_convffn_trunk"


def rmsnorm(x, g):
    xf = x.astype(jnp.float32)
    y = xf * lax.rsqrt(jnp.mean(xf * xf, axis=-1, keepdims=True) + EPS) * g.astype(jnp.float32)
    return y.astype(x.dtype)


def rope_tables(length):
    half = HEAD_DIM // 2
    inv_freq = ROPE_THETA ** (-jnp.arange(half, dtype=jnp.float32) * 2.0 / HEAD_DIM)
    ang = jnp.arange(length, dtype=jnp.float32)[:, None] * inv_freq[None, :]
    return jnp.cos(ang), jnp.sin(ang)


def apply_rope(x, cos, sin):
    half = HEAD_DIM // 2
    xf = x.astype(jnp.float32)
    x1, x2 = xf[..., :half], xf[..., half:]
    c, s = cos[None, :, None, :], sin[None, :, None, :]
    return jnp.concatenate([x1 * c - x2 * s, x1 * s + x2 * c], axis=-1).astype(x.dtype)


def s5_mixer(xn, A_re, A_im, log_dt, B_re, B_im, C_re, C_im, D_skip, w_glu, b_glu):
    bsz, length, d = xn.shape
    u = xn.astype(jnp.float32)
    lam = lax.complex(A_re.astype(jnp.float32), A_im.astype(jnp.float32))
    dt = jnp.exp(log_dt.astype(jnp.float32))[:, None]
    lam_bar = jnp.exp(lam * dt)
    b_mat = lax.complex(B_re.astype(jnp.float32), B_im.astype(jnp.float32))
    b_bar = ((lam_bar - 1.0) / lam)[..., None] * b_mat
    ug = u.reshape(bsz, length, S5_GROUPS, S5_GROUP).astype(jnp.complex64)
    bu = jnp.einsum('blgc,gpc->blgp', ug, b_bar)
    a = jnp.broadcast_to(lam_bar, bu.shape)

    def combine(left, right):
        a_l, b_l = left
        a_r, b_r = right
        return a_r * a_l, a_r * b_l + b_r

    _, states = lax.associative_scan(combine, (a, bu), axis=1)
    c_mat = lax.complex(C_re.astype(jnp.float32), C_im.astype(jnp.float32))
    y = jnp.real(jnp.einsum('blgp,gcp->blgc', states, c_mat)).reshape(bsz, length, d)
    y = y + D_skip.astype(jnp.float32) * u
    g = jax.nn.gelu(y).astype(xn.dtype)
    z = g @ w_glu + b_glu
    za, zb = jnp.split(z, 2, axis=-1)
    return za * jax.nn.sigmoid(zb)


def conv_ffn(xn, w_up, conv_w, conv_b, w_down):
    h = xn @ w_up
    ch = h.shape[-1]
    h = lax.conv_general_dilated(h, conv_w[:, None, :], window_strides=(1,),
                                 padding=[(CONV_WIDTH - 1, 0)],
                                 dimension_numbers=('NWC', 'WIO', 'NWC'),
                                 feature_group_count=ch) + conv_b
    gate, val = jnp.split(h, 2, axis=-1)
    return (jax.nn.silu(gate) * val) @ w_down


def pad_to_blocks(t, n_blocks):
    length = t.shape[1]
    t = jnp.pad(t, ((0, 0), (0, n_blocks * MOBA_BLOCK - length), (0, 0), (0, 0)))
    return t.transpose(0, 2, 1, 3)


def shared_kv(h, kv_norm, w_kv, k_norm, cos, sin):
    bsz, length, _ = h.shape
    n_blocks = -(-length // MOBA_BLOCK)
    kv = rmsnorm(h, kv_norm) @ w_kv
    k, v = jnp.split(kv, 2, axis=-1)
    k = apply_rope(rmsnorm(k.reshape(bsz, length, N_HEADS, HEAD_DIM), k_norm), cos, sin)
    v = v.reshape(bsz, length, N_HEADS, HEAD_DIM)
    kb = pad_to_blocks(k, n_blocks).reshape(bsz, N_HEADS, n_blocks, MOBA_BLOCK, HEAD_DIM)
    vb = pad_to_blocks(v, n_blocks).reshape(bsz, N_HEADS, n_blocks, MOBA_BLOCK, HEAD_DIM)
    kmean = jnp.mean(kb.astype(jnp.float32), axis=3)
    return kb, vb, kmean


def moba_attention(q, kb, vb, kmean):
    bsz, nh, n_blocks, blk, dh = kb.shape
    n_chunks = n_blocks * blk // QUERY_CHUNK
    k_sel = min(MOBA_TOPK, n_blocks)
    scale = dh ** -0.5
    kflat = kb.reshape(bsz, nh, n_blocks * blk, dh)
    vflat = vb.reshape(bsz, nh, n_blocks * blk, dh)
    qc = q.reshape(bsz, nh, n_chunks, QUERY_CHUNK, dh).transpose(2, 0, 1, 3, 4)
    bidx = jnp.arange(bsz)[:, None, None, None]
    hidx = jnp.arange(nh)[None, :, None, None]

    def one_chunk(args):
        qi, ci = args
        start = ci * QUERY_CHUNK
        cur = start // blk
        qpos = start + jnp.arange(QUERY_CHUNK)
        gate = jnp.einsum('bhcd,bhnd->bhcn', qi.astype(jnp.float32), kmean)
        gate = jnp.where(jnp.arange(n_blocks) < cur, gate, -jnp.inf)
        _, sel = lax.top_k(gate, k_sel)
        sel_valid = sel < cur
        kg = kb[bidx, hidx, sel]
        vg = vb[bidx, hidx, sel]
        s_sel = jnp.einsum('bhcd,bhcknd->bhckn', qi, kg).astype(jnp.float32) * scale
        s_sel = jnp.where(sel_valid[..., None], s_sel, -jnp.inf)
        s_sel = s_sel.reshape(bsz, nh, QUERY_CHUNK, k_sel * blk)
        kown = lax.dynamic_slice_in_dim(kflat, cur * blk, blk, axis=2)
        vown = lax.dynamic_slice_in_dim(vflat, cur * blk, blk, axis=2)
        kpos = cur * blk + jnp.arange(blk)
        s_own = jnp.einsum('bhcd,bhnd->bhcn', qi, kown).astype(jnp.float32) * scale
        s_own = jnp.where(kpos[None, :] <= qpos[:, None], s_own, -jnp.inf)
        p = jax.nn.softmax(jnp.concatenate([s_sel, s_own], axis=-1), axis=-1)
        p_sel = p[..., :k_sel * blk].reshape(bsz, nh, QUERY_CHUNK, k_sel, blk).astype(vb.dtype)
        p_own = p[..., k_sel * blk:].astype(vb.dtype)
        return (jnp.einsum('bhckn,bhcknd->bhcd', p_sel, vg)
                + jnp.einsum('bhcn,bhnd->bhcd', p_own, vown))

    outs = lax.map(one_chunk, (qc, jnp.arange(n_chunks)))
    return outs.transpose(1, 2, 0, 3, 4).reshape(bsz, nh, n_chunks * QUERY_CHUNK, dh)


def moba_mixer(hn, w_q, q_norm, w_o, kb, vb, kmean, cos, sin):
    bsz, length, d = hn.shape
    n_blocks = kb.shape[2]
    q = (hn @ w_q).reshape(bsz, length, N_HEADS, HEAD_DIM)
    q = apply_rope(rmsnorm(q, q_norm), cos, sin)
    q = pad_to_blocks(q, n_blocks)
    o = moba_attention(q, kb, vb, kmean)[:, :, :length]
    o = o.transpose(0, 2, 1, 3).reshape(bsz, length, d)
    return o @ w_o


def setup_inputs(seed: int = 0) -> dict:
    key = jax.random.key(seed)
    ks = iter(jax.random.split(key, 32))
    nrm = lambda shape, s: jax.random.normal(next(ks), shape, jnp.float32) * s
    na, nb, G, P, C = N_A_LAYERS, N_B_LAYERS, S5_GROUPS, S5_STATE, S5_GROUP
    x = nrm((BATCH, SEQ, D_MODEL), 1.0)
    a_norm = 1.0 + nrm((na, D_MODEL), 0.02)
    s5_A_re = -0.5 + nrm((na, G, P), 0.01)
    s5_A_im = math.pi * jnp.arange(P, dtype=jnp.float32)[None, None, :] + nrm((na, G, P), 0.01)
    s5_log_dt = jax.random.uniform(next(ks), (na, G), jnp.float32, math.log(DT_MIN), math.log(DT_MAX))
    s5_B_re = nrm((na, G, P, C), (2 * C) ** -0.5)
    s5_B_im = nrm((na, G, P, C), (2 * C) ** -0.5)
    s5_C_re = nrm((na, G, C, P), P ** -0.5)
    s5_C_im = nrm((na, G, C, P), P ** -0.5)
    s5_D = 1.0 + nrm((na, D_MODEL), 0.1)
    w_glu = nrm((na, D_MODEL, 2 * D_MODEL), D_MODEL ** -0.5)
    b_glu = nrm((na, 2 * D_MODEL), 0.01)
    kv_norm = 1.0 + nrm((D_MODEL,), 0.02)
    w_kv = nrm((D_MODEL, 2 * N_HEADS * HEAD_DIM), D_MODEL ** -0.5)
    k_norm = 1.0 + nrm((HEAD_DIM,), 0.02)
    b_norm = 1.0 + nrm((nb, D_MODEL), 0.02)
    w_q = nrm((nb, D_MODEL, N_HEADS * HEAD_DIM), D_MODEL ** -0.5)
    q_norm = 1.0 + nrm((nb, HEAD_DIM), 0.02)
    w_o = nrm((nb, N_HEADS * HEAD_DIM, D_MODEL), (N_HEADS * HEAD_DIM) ** -0.5)
    ffn_norm = 1.0 + nrm((DEPTH, D_MODEL), 0.02)
    w_up = nrm((DEPTH, D_MODEL, 2 * D_FF), D_MODEL ** -0.5)
    conv_w = nrm((DEPTH, CONV_WIDTH, 2 * D_FF), CONV_WIDTH ** -0.5)
    conv_b = nrm((DEPTH, 2 * D_FF), 0.01)
    w_down = nrm((DEPTH, D_FF, D_MODEL), D_FF ** -0.5)
    return {"x": x, "a_norm": a_norm, "s5_A_re": s5_A_re, "s5_A_im": s5_A_im,
            "s5_log_dt": s5_log_dt, "s5_B_re": s5_B_re, "s5_B_im": s5_B_im,
            "s5_C_re": s5_C_re, "s5_C_im": s5_C_im, "s5_D": s5_D, "w_glu": w_glu,
            "b_glu": b_glu, "kv_norm": kv_norm, "w_kv": w_kv, "k_norm": k_norm,
            "b_norm": b_norm, "w_q": w_q, "q_norm": q_norm, "w_o": w_o,
            "ffn_norm": ffn_norm, "w_up": w_up, "conv_w": conv_w, "conv_b": conv_b,
            "w_down": w_down}


def reference(x, a_norm, s5_A_re, s5_A_im, s5_log_dt, s5_B_re, s5_B_im, s5_C_re, s5_C_im,
              s5_D, w_glu, b_glu, kv_norm, w_kv, k_norm, b_norm, w_q, q_norm, w_o,
              ffn_norm, w_up, conv_w, conv_b, w_down):
    length = x.shape[1]
    cos, sin = rope_tables(length)
    h = x
    kb = vb = kmean = None
    for layer in range(DEPTH):
        if layer < N_A_LAYERS:
            i = layer
            h = h + s5_mixer(rmsnorm(h, a_norm[i]), s5_A_re[i], s5_A_im[i], s5_log_dt[i],
                             s5_B_re[i], s5_B_im[i], s5_C_re[i], s5_C_im[i], s5_D[i],
                             w_glu[i], b_glu[i]).astype(h.dtype)
        else:
            if layer == N_A_LAYERS:
                kb, vb, kmean = shared_kv(h, kv_norm, w_kv, k_norm, cos, sin)
            j = layer - N_A_LAYERS
            h = h + moba_mixer(rmsnorm(h, b_norm[j]), w_q[j], q_norm[j], w_o[j],
                               kb, vb, kmean, cos, sin).astype(h.dtype)
        h = h + conv_ffn(rmsnorm(h, ffn_norm[layer]), w_up[layer], conv_w[layer],
                         conv_b[layer], w_down[layer]).astype(h.dtype)
    return h
```

```python
import functools
import math

import jax
import jax.numpy as jnp
from jax import lax
from jax.experimental import pallas as pl
from jax.experimental.pallas import tpu as pltpu

MOBA_BLOCK = 256
MOBA_TOPK = 3
CONV_WIDTH = 3
ROPE_THETA = 10000.0
EPS = 1e-6

V7X_LANES = 128
V7X_SUBLANES = 8
V7X_MXU_DIM = 256
V7X_VMEM_BYTES = 64 * 1024 * 1024

MASK_BIAS = -1e30

F32 = jnp.float32
BF16 = jnp.bfloat16


def _vmem_limit(est_bytes):
    return int(min(V7X_VMEM_BYTES - 6 * 1024 * 1024, max(32 * 1024 * 1024, 2 * est_bytes)))


def _params(semantics, est_bytes):
    return pltpu.CompilerParams(dimension_semantics=semantics,
                                vmem_limit_bytes=_vmem_limit(est_bytes))


def _resident(shape):
    nd = len(shape)
    return pl.BlockSpec(shape, lambda *_: (0,) * nd, pipeline_mode=pl.Buffered(1))


def _rmsnorm(x, g):
    return x * lax.rsqrt(jnp.mean(x * x, axis=-1, keepdims=True) + EPS) * g


def _s5_kernel(h_ref, anorm_ref, wb_ref, lam_ref, wc_ref, dskip_ref, wglu_ref, bglu_ref,
               out_ref, state_ref, bu_ref, xn_ref, xnb_ref, g_ref,
               *, n_tiles, tile_ch, nstate, nbatch, tsteps, unroll):
    @pl.when(pl.program_id(0) == 0)
    def _():
        state_ref[...] = jnp.zeros_like(state_ref)

    xn = _rmsnorm(h_ref[...], anorm_ref[...])
    xn_ref[...] = xn
    xnb_ref[...] = xn.astype(BF16)

    for j in range(n_tiles):
        cs = pl.ds(j * tile_ch, tile_ch)
        bu_ref[...] = jnp.dot(xnb_ref[:, cs], wb_ref[j], preferred_element_type=F32)
        lr = jnp.broadcast_to(lam_ref[j, 0:1, :], (nbatch, nstate))
        li = jnp.broadcast_to(lam_ref[j, 1:2, :], (nbatch, nstate))

        def step(t, carry, lr=lr, li=li):
            sr, si = carry
            rows = pl.ds(pl.multiple_of(t * nbatch, nbatch), nbatch)
            br = bu_ref[rows, pl.ds(0, nstate)]
            bi = bu_ref[rows, pl.ds(nstate, nstate)]
            nr = lr * sr - li * si + br
            ni = lr * si + li * sr + bi
            bu_ref[rows, pl.ds(0, nstate)] = nr
            bu_ref[rows, pl.ds(nstate, nstate)] = ni
            return nr, ni

        sr, si = lax.fori_loop(
            0, tsteps, step,
            (state_ref[j, :, pl.ds(0, nstate)], state_ref[j, :, pl.ds(nstate, nstate)]),
            unroll=unroll)
        state_ref[j, :, pl.ds(0, nstate)] = sr
        state_ref[j, :, pl.ds(nstate, nstate)] = si

        y = jnp.dot(bu_ref[...].astype(BF16), wc_ref[j], preferred_element_type=F32)
        y = y + dskip_ref[:, cs] * xn_ref[:, cs]
        g_ref[:, cs] = jax.nn.gelu(y).astype(BF16)

    z = jnp.dot(g_ref[...], wglu_ref[...], preferred_element_type=F32) + bglu_ref[...]
    d = out_ref.shape[-1]
    out_ref[...] = h_ref[...] + z[:, :d] * jax.nn.sigmoid(z[:, d:])


def _s5_discretise(a_re, a_im, log_dt, b_re, b_im, c_re, c_im, group_tile):
    g, p = a_re.shape
    c = b_re.shape[-1]
    lam = lax.complex(a_re, a_im)
    dt = jnp.exp(log_dt)[:, None]
    lam_bar = jnp.exp(lam * dt)
    b_bar = ((lam_bar - 1.0) / lam)[..., None] * lax.complex(b_re, b_im)
    n_tiles = g // group_tile
    eye = jnp.eye(group_tile, dtype=F32)

    def pack_in(m):
        m = m.reshape(n_tiles, group_tile, p, c)
        w = jnp.einsum('jgpc,gh->jgchp', m, eye)
        return w.reshape(n_tiles, group_tile * c, group_tile * p)

    def pack_out(m):
        m = m.reshape(n_tiles, group_tile, c, p)
        w = jnp.einsum('jgcp,gh->jgphc', m, eye)
        return w.reshape(n_tiles, group_tile * p, group_tile * c)

    wb = jnp.concatenate([pack_in(jnp.real(b_bar)), pack_in(jnp.imag(b_bar))], axis=-1)
    wc = jnp.concatenate([pack_out(c_re), pack_out(-c_im)], axis=1)
    lam_t = jnp.stack([jnp.real(lam_bar).reshape(n_tiles, group_tile * p),
                       jnp.imag(lam_bar).reshape(n_tiles, group_tile * p)], axis=1)
    return wb.astype(BF16), wc.astype(BF16), lam_t


def _s5_layer(h, nbatch, a_norm, a_re, a_im, log_dt, b_re, b_im, c_re, c_im, d_skip,
              w_glu, b_glu):
    rows, d = h.shape
    length = rows // nbatch
    g, p = a_re.shape
    c = d // g
    group_tile = min(g, V7X_MXU_DIM // c)
    tile_ch = group_tile * c
    nstate = group_tile * p
    n_tiles = g // group_tile
    tsteps = min(64, length)
    tb = tsteps * nbatch
    wb, wc, lam_t = _s5_discretise(a_re, a_im, log_dt, b_re, b_im, c_re, c_im, group_tile)
    est = (4 * tb * d * 4 + tb * 2 * nstate * 4 + tb * d * 8 + wb.size * 2 + wc.size * 2
           + w_glu.size * 2 + 2 * tb * 2 * d * 4)
    kern = functools.partial(_s5_kernel, n_tiles=n_tiles, tile_ch=tile_ch, nstate=nstate,
                             nbatch=nbatch, tsteps=tsteps, unroll=min(8, tsteps))
    return pl.pallas_call(
        kern,
        grid=(length // tsteps,),
        in_specs=[
            pl.BlockSpec((tb, d), lambda i: (i, 0)),
            _resident((1, d)),
            _resident(wb.shape),
            _resident(lam_t.shape),
            _resident(wc.shape),
            _resident((1, d)),
            _resident(w_glu.shape),
            _resident((1, 2 * d)),
        ],
        out_specs=pl.BlockSpec((tb, d), lambda i: (i, 0)),
        out_shape=jax.ShapeDtypeStruct((rows, d), F32),
        scratch_shapes=[
            pltpu.VMEM((n_tiles, nbatch, 2 * nstate), F32),
            pltpu.VMEM((tb, 2 * nstate), F32),
            pltpu.VMEM((tb, d), F32),
            pltpu.VMEM((tb, d), BF16),
            pltpu.VMEM((tb, d), BF16),
        ],
        compiler_params=_params(("arbitrary",), est),
        name="s5_mixer",
    )(h, a_norm.reshape(1, d), wb, lam_t, wc, d_skip.reshape(1, d),
      w_glu.astype(BF16), b_glu.reshape(1, 2 * d))


def _ffn_kernel(*refs, has_attn, shift, halo, tiles_per_seq, n_fc, tm):
    if has_attn:
        (h_ref, o_ref, wo_ref, fnorm_ref, wup_ref, cw_ref, cb_ref, wdown_ref,
         out_ref, carry_ref, hu_ref, xnb_ref, acc_ref) = refs
    else:
        (h_ref, fnorm_ref, wup_ref, cw_ref, cb_ref, wdown_ref,
         out_ref, carry_ref, hu_ref, xnb_ref, acc_ref) = refs

    @pl.when(pl.program_id(0) % tiles_per_seq == 0)
    def _():
        carry_ref[...] = jnp.zeros_like(carry_ref)

    h1 = h_ref[...]
    if has_attn:
        h1 = h1 + jnp.dot(o_ref[...], wo_ref[...], preferred_element_type=F32)
    out_ref[...] = h1
    xnb_ref[...] = _rmsnorm(h1, fnorm_ref[...]).astype(BF16)
    acc_ref[...] = jnp.zeros_like(acc_ref)

    def conv_branch(idx):
        hu_ref[pl.ds(0, halo), :] = carry_ref[idx]
        hu_ref[pl.ds(halo, tm), :] = jnp.dot(xnb_ref[...], wup_ref[idx],
                                             preferred_element_type=F32)
        carry_ref[idx] = hu_ref[pl.ds(tm, halo), :]
        w = cw_ref[idx]
        return (w[2:3, :] * hu_ref[pl.ds(halo, tm), :]
                + w[1:2, :] * hu_ref[pl.ds(halo - shift, tm), :]
                + w[0:1, :] * hu_ref[pl.ds(halo - 2 * shift, tm), :]
                + cb_ref[idx])

    def chunk(c, carry):
        gate = conv_branch(c)
        val = conv_branch(n_fc + c)
        act = (jax.nn.silu(gate) * val).astype(BF16)
        acc_ref[...] += jnp.dot(act, wdown_ref[c], preferred_element_type=F32)
        return carry

    lax.fori_loop(0, n_fc, chunk, 0)
    out_ref[...] = out_ref[...] + acc_ref[...]


def _ffn_layer(h, h_map, n_rows, d, shift, tiles_per_seq, tm, ffn_norm, w_up, conv_w, conv_b,
               w_down, attn=None):
    f = w_down.shape[0]
    fc = V7X_MXU_DIM
    n_fc = f // fc
    halo = -(-2 * shift // V7X_SUBLANES) * V7X_SUBLANES
    wup = w_up.astype(BF16).reshape(d, 2 * n_fc, fc).transpose(1, 0, 2)
    cw = conv_w.reshape(CONV_WIDTH, 2 * n_fc, fc).transpose(1, 0, 2)
    cb = conv_b.reshape(2 * n_fc, 1, fc)
    wdown = w_down.astype(BF16).reshape(n_fc, fc, d)
    row_map = lambda i: (i, 0)
    in_specs = [pl.BlockSpec((tm, d), h_map)]
    args = [h]
    if attn is not None:
        o, w_o = attn
        in_specs += [pl.BlockSpec((tm, d), row_map), _resident(w_o.shape)]
        args += [o, w_o.astype(BF16)]
    in_specs += [_resident((1, d)), _resident(wup.shape), _resident(cw.shape),
                 _resident(cb.shape), _resident(wdown.shape)]
    args += [ffn_norm.reshape(1, d), wup, cw, cb, wdown]
    est = (wup.size * 2 + wdown.size * 2 + d * d * 2 + 6 * tm * d * 4 + tm * d * 6
           + (tm + halo) * fc * 4 + 2 * n_fc * halo * fc * 4 + 4 * tm * fc * 4)
    kern = functools.partial(_ffn_kernel, has_attn=attn is not None, shift=shift, halo=halo,
                             tiles_per_seq=tiles_per_seq, n_fc=n_fc, tm=tm)
    return pl.pallas_call(
        kern,
        grid=(n_rows // tm,),
        in_specs=in_specs,
        out_specs=pl.BlockSpec((tm, d), row_map),
        out_shape=jax.ShapeDtypeStruct((n_rows, d), F32),
        scratch_shapes=[
            pltpu.VMEM((2 * n_fc, halo, fc), F32),
            pltpu.VMEM((tm + halo, fc), F32),
            pltpu.VMEM((tm, d), BF16),
            pltpu.VMEM((tm, d), F32),
        ],
        compiler_params=_params(("arbitrary",), est),
        name="conv_ffn_attn" if attn is not None else "conv_ffn",
    )(*args)


def _rope(x, cc, ss, dh):
    return x * cc + pltpu.roll(x, dh // 2, axis=1) * ss


def _kv_kernel(h_ref, nrm_ref, wkv_ref, knorm_ref, cc_ref, ss_ref,
               kaug_ref, v_ref, kmean_ref, *, n_heads, dh):
    xn = _rmsnorm(h_ref[...], nrm_ref[...]).astype(BF16)
    kv = jnp.dot(xn, wkv_ref[...], preferred_element_type=F32)
    rows = kv.shape[0]
    blk = pl.program_id(1)
    lane = lax.broadcasted_iota(jnp.int32, (rows, dh), 1)
    onehot = jnp.where(lane == blk, 1.0, 0.0).astype(BF16)
    cc = cc_ref[...]
    ss = ss_ref[...]
    means = []
    for hd in range(n_heads):
        kh = _rmsnorm(kv[:, hd * dh:(hd + 1) * dh], knorm_ref[...])
        kr = _rope(kh, cc, ss, dh)
        kaug_ref[0, hd, :, pl.ds(0, dh)] = kr.astype(BF16)
        kaug_ref[0, hd, :, pl.ds(dh, dh)] = onehot
        v_ref[0, hd] = kv[:, (n_heads + hd) * dh:(n_heads + hd + 1) * dh].astype(BF16)
        means.append(jnp.mean(kr, axis=0, keepdims=True))
    kmean_ref[0, 0] = jnp.concatenate(means, axis=0)


def _kv_proj(h_view, nbatch, length, d, kv_norm, w_kv, k_norm, cc, ss):
    dh = k_norm.shape[0]
    n_heads = w_kv.shape[1] // (2 * dh)
    nblk = length // MOBA_BLOCK
    t = MOBA_BLOCK
    est = w_kv.size * 2 + 2 * t * d * 4 + 3 * t * 2 * n_heads * dh * 4 + 6 * n_heads * t * dh * 2
    return pl.pallas_call(
        functools.partial(_kv_kernel, n_heads=n_heads, dh=dh),
        grid=(nbatch, nblk),
        in_specs=[
            pl.BlockSpec((t, d), lambda b, i: (i, b)),
            _resident((1, d)),
            _resident(w_kv.shape),
            _resident((1, dh)),
            pl.BlockSpec((t, dh), lambda b, i: (i, 0)),
            pl.BlockSpec((t, dh), lambda b, i: (i, 0)),
        ],
        out_specs=[
            pl.BlockSpec((1, n_heads, t, 2 * dh), lambda b, i: (b, 0, i, 0)),
            pl.BlockSpec((1, n_heads, t, dh), lambda b, i: (b, 0, i, 0)),
            pl.BlockSpec((1, 1, n_heads, dh), lambda b, i: (b, i, 0, 0)),
        ],
        out_shape=[
            jax.ShapeDtypeStruct((nbatch, n_heads, length, 2 * dh), BF16),
            jax.ShapeDtypeStruct((nbatch, n_heads, length, dh), BF16),
            jax.ShapeDtypeStruct((nbatch, nblk, n_heads, dh), F32),
        ],
        compiler_params=_params(("parallel", "parallel"), est),
        name="kv_proj",
    )(h_view, kv_norm.reshape(1, d), w_kv.astype(BF16), k_norm.reshape(1, dh), cc, ss)


def _q_kernel(h_ref, nrm_ref, wq_ref, qnorm_ref, cc_ref, ss_ref, kmean_ref, qaug_ref,
              *, n_heads, dh, nblk, topk, scale):
    xn = _rmsnorm(h_ref[...], nrm_ref[...]).astype(BF16)
    q = jnp.dot(xn, wq_ref[...], preferred_element_type=F32)
    rows = q.shape[0]
    cur = pl.program_id(1)
    cc = cc_ref[...]
    ss = ss_ref[...]
    blk_id = lax.broadcasted_iota(jnp.int32, (nblk, rows), 0)
    for hd in range(n_heads):
        qh = _rmsnorm(q[:, hd * dh:(hd + 1) * dh], qnorm_ref[...])
        qr = _rope(qh, cc, ss, dh)
        gate_t = lax.dot_general(kmean_ref[0, hd], qr, (((1,), (1,)), ((), ())),
                                 precision=lax.Precision.HIGHEST,
                                 preferred_element_type=F32)
        gate_t = jnp.where(blk_id < cur, gate_t, -jnp.inf)
        sel_rows = []
        for j in range(nblk):
            gj = gate_t[j:j + 1, :]
            beats = jnp.where((gate_t > gj) | ((gate_t == gj) & (blk_id < j)), 1.0, 0.0)
            rank = jnp.sum(beats, axis=0, keepdims=True)
            sel_rows.append(jnp.where((rank < topk) & (j < cur), 1.0, 0.0))
        sel_t = jnp.concatenate(
            sel_rows + [jnp.zeros((dh - nblk, rows), F32)], axis=0)
        sel = sel_t.T
        qaug_ref[0, hd, :, pl.ds(0, dh)] = (qr * scale).astype(BF16)
        qaug_ref[0, hd, :, pl.ds(dh, dh)] = jnp.where(sel > 0.5, 0.0, MASK_BIAS).astype(BF16)


def _q_proj(h, h_map, nbatch, length, d, b_norm, w_q, q_norm, cc, ss, kmean):
    dh = q_norm.shape[0]
    n_heads = w_q.shape[1] // dh
    nblk = length // MOBA_BLOCK
    t = MOBA_BLOCK
    est = w_q.size * 2 + 2 * t * d * 4 + 3 * t * n_heads * dh * 4 + 4 * n_heads * t * dh * 2
    kern = functools.partial(_q_kernel, n_heads=n_heads, dh=dh, nblk=nblk,
                             topk=min(MOBA_TOPK, nblk), scale=dh ** -0.5)
    return pl.pallas_call(
        kern,
        grid=(nbatch, nblk),
        in_specs=[
            pl.BlockSpec((t, d), h_map),
            _resident((1, d)),
            _resident(w_q.shape),
            _resident((1, dh)),
            pl.BlockSpec((t, dh), lambda b, i: (i, 0)),
            pl.BlockSpec((t, dh), lambda b, i: (i, 0)),
            pl.BlockSpec((1, n_heads, nblk, dh), lambda b, i: (b, 0, 0, 0)),
        ],
        out_specs=pl.BlockSpec((1, n_heads, t, 2 * dh), lambda b, i: (b, 0, i, 0)),
        out_shape=jax.ShapeDtypeStruct((nbatch, n_heads, length, 2 * dh), BF16),
        compiler_params=_params(("parallel", "parallel"), est),
        name="q_proj_gate",
    )(h, b_norm.reshape(1, d), w_q.astype(BF16), q_norm.reshape(1, dh), cc, ss, kmean)


def _attn_kernel(q_ref, k_ref, v_ref, o_ref, m_ref, l_ref, acc_ref, *, blk, dh):
    cur = pl.program_id(2)
    nt = (((1,), (1,)), ((), ()))

    own = pl.ds(pl.multiple_of(cur * blk, blk), blk)
    k_own = k_ref[0, 0, own, :]
    s = lax.dot_general(q_ref[0, 0, :, pl.ds(0, dh)], k_own[:, :dh], nt,
                        preferred_element_type=F32)
    qpos = lax.broadcasted_iota(jnp.int32, s.shape, 0)
    kpos = lax.broadcasted_iota(jnp.int32, s.shape, 1)
    s = jnp.where(kpos <= qpos, s, -jnp.inf)
    m = jnp.max(s, axis=1, keepdims=True)
    p = jnp.exp(s - m)
    m_ref[...] = m
    l_ref[...] = jnp.sum(p, axis=1, keepdims=True)
    acc_ref[...] = jnp.dot(p.astype(BF16), v_ref[0, 0, own, :], preferred_element_type=F32)

    def past(j, carry):
        rows = pl.ds(pl.multiple_of(j * blk, blk), blk)
        s = lax.dot_general(q_ref[0, 0], k_ref[0, 0, rows, :], nt, preferred_element_type=F32)
        m_old = m_ref[...]
        m_new = jnp.maximum(m_old, jnp.max(s, axis=1, keepdims=True))
        alpha = jnp.exp(m_old - m_new)
        p = jnp.exp(s - m_new)
        m_ref[...] = m_new
        l_ref[...] = alpha * l_ref[...] + jnp.sum(p, axis=1, keepdims=True)
        acc_ref[...] = alpha * acc_ref[...] + jnp.dot(p.astype(BF16), v_ref[0, 0, rows, :],
                                                      preferred_element_type=F32)
        return carry

    lax.fori_loop(0, cur, past, 0)
    o_ref[...] = (acc_ref[...] / l_ref[...]).astype(o_ref.dtype)


def _attention(qaug, kaug, v):
    nbatch, n_heads, length, dh2 = qaug.shape
    dh = dh2 // 2
    blk = MOBA_BLOCK
    nblk = length // blk
    est = 2 * length * dh2 * 2 + 2 * length * dh * 2 + 4 * blk * dh2 * 2 + 8 * blk * blk * 4
    return pl.pallas_call(
        functools.partial(_attn_kernel, blk=blk, dh=dh),
        grid=(nbatch, n_heads, nblk),
        in_specs=[
            pl.BlockSpec((1, 1, blk, dh2), lambda b, h, i: (b, h, i, 0)),
            pl.BlockSpec((1, 1, length, dh2), lambda b, h, i: (b, h, 0, 0)),
            pl.BlockSpec((1, 1, length, dh), lambda b, h, i: (b, h, 0, 0)),
        ],
        out_specs=pl.BlockSpec((blk, dh), lambda b, h, i: (b * nblk + i, h)),
        out_shape=jax.ShapeDtypeStruct((nbatch * length, n_heads * dh), BF16),
        scratch_shapes=[
            pltpu.VMEM((blk, 1), F32),
            pltpu.VMEM((blk, 1), F32),
            pltpu.VMEM((blk, dh), F32),
        ],
        compiler_params=_params(("parallel", "parallel", "parallel"), est),
        name="moba_attention",
    )(qaug, kaug, v)


def _rope_tables(length, dh):
    half = dh // 2
    inv_freq = ROPE_THETA ** (-jnp.arange(half, dtype=F32) * 2.0 / dh)
    ang = jnp.arange(length, dtype=F32)[:, None] * inv_freq[None, :]
    cos, sin = jnp.cos(ang), jnp.sin(ang)
    return jnp.concatenate([cos, cos], axis=-1), jnp.concatenate([-sin, sin], axis=-1)


def kernel(x, a_norm, s5_A_re, s5_A_im, s5_log_dt, s5_B_re, s5_B_im, s5_C_re, s5_C_im, s5_D,
           w_glu, b_glu, kv_norm, w_kv, k_norm, b_norm, w_q, q_norm, w_o, ffn_norm, w_up,
           conv_w, conv_b, w_down):
    nbatch, length, d = x.shape
    n_a = a_norm.shape[0]
    n_b = b_norm.shape[0]
    rows = nbatch * length
    assert nbatch == V7X_SUBLANES, "the S5 recurrence keeps the batch on the sublane axis"
    assert length % MOBA_BLOCK == 0 and n_a >= 1 and n_b >= 1
    dh = k_norm.shape[0]
    cc, ss = _rope_tables(length, dh)

    h = x.transpose(1, 0, 2).reshape(rows, d)
    tm_a = min(512, rows)
    for i in range(n_a):
        h = _s5_layer(h, nbatch, a_norm[i], s5_A_re[i], s5_A_im[i], s5_log_dt[i], s5_B_re[i],
                      s5_B_im[i], s5_C_re[i], s5_C_im[i], s5_D[i], w_glu[i], b_glu[i])
        h = _ffn_layer(h, lambda n: (n, 0), rows, d, nbatch, rows // tm_a, tm_a,
                       ffn_norm[i], w_up[i], conv_w[i], conv_b[i], w_down[i])

    h_view = h.reshape(length, nbatch * d)
    kaug, v, kmean = _kv_proj(h_view, nbatch, length, d, kv_norm, w_kv, k_norm, cc, ss)
    kmean = kmean.transpose(0, 2, 1, 3)
    tm_b = min(512, length)
    tps = length // tm_b
    for j in range(n_b):
        layer = n_a + j
        if j == 0:
            src, q_map, f_map = h_view, (lambda b, i: (i, b)), (lambda n: (n % tps, n // tps))
        else:
            nblk = length // MOBA_BLOCK
            src, q_map, f_map = h, (lambda b, i: (b * nblk + i, 0)), (lambda n: (n, 0))
        qaug = _q_proj(src, q_map, nbatch, length, d, b_norm[j], w_q[j], q_norm[j], cc, ss, kmean)
        o = _attention(qaug, kaug, v)
        h = _ffn_layer(src, f_map, rows, d, 1, tps, tm_b, ffn_norm[layer], w_up[layer],
                       conv_w[layer], conv_b[layer], w_down[layer], attn=(o, w_o[j]))
    return h.reshape(nbatch, length, d)
```

```python
import functools
import math

import jax
import jax.numpy as jnp
from jax import lax
from jax.experimental import pallas as pl
from jax.experimental.pallas import tpu as pltpu

MOBA_BLOCK = 256
MOBA_TOPK = 3
CONV_WIDTH = 3
ROPE_THETA = 10000.0
EPS = 1e-6

V7X_LANES = 128
V7X_SUBLANES = 8
V7X_MXU_DIM = 256
V7X_VMEM_BYTES = 64 * 1024 * 1024

MASK_BIAS = -1e30

F32 = jnp.float32
BF16 = jnp.bfloat16


def _vmem_limit(est_bytes):
    return int(min(V7X_VMEM_BYTES - 6 * 1024 * 1024, max(32 * 1024 * 1024, 2 * est_bytes)))


def _params(semantics, est_bytes):
    return pltpu.CompilerParams(dimension_semantics=semantics,
                                vmem_limit_bytes=_vmem_limit(est_bytes))


def _resident(shape):
    nd = len(shape)
    return pl.BlockSpec(shape, lambda *_: (0,) * nd, pipeline_mode=pl.Buffered(1))


def _rmsnorm(x, g):
    return x * lax.rsqrt(jnp.mean(x * x, axis=-1, keepdims=True) + EPS) * g


def _s5_kernel(h_ref, anorm_ref, wb_ref, lam_ref, wc_ref, dskip_ref, wglu_ref, bglu_ref,
               out_ref, state_ref, bu_ref, xn_ref, xnb_ref, g_ref,
               *, n_tiles, tile_ch, nstate, nbatch, tsteps, unroll):
    @pl.when(pl.program_id(0) == 0)
    def _():
        state_ref[...] = jnp.zeros_like(state_ref)

    xn = _rmsnorm(h_ref[...], anorm_ref[...])
    xn_ref[...] = xn
    xnb_ref[...] = xn.astype(BF16)

    for j in range(n_tiles):
        cs = pl.ds(j * tile_ch, tile_ch)
        bu_ref[...] = jnp.dot(xnb_ref[:, cs], wb_ref[j], preferred_element_type=F32)
        lr = jnp.broadcast_to(lam_ref[j, 0:1, :], (nbatch, nstate))
        li = jnp.broadcast_to(lam_ref[j, 1:2, :], (nbatch, nstate))

        def step(t, carry, lr=lr, li=li):
            sr, si = carry
            rows = pl.ds(pl.multiple_of(t * nbatch, nbatch), nbatch)
            br = bu_ref[rows, pl.ds(0, nstate)]
            bi = bu_ref[rows, pl.ds(nstate, nstate)]
            nr = lr * sr - li * si + br
            ni = lr * si + li * sr + bi
            bu_ref[rows, pl.ds(0, nstate)] = nr
            bu_ref[rows, pl.ds(nstate, nstate)] = ni
            return nr, ni

        sr, si = lax.fori_loop(
            0, tsteps, step,
            (state_ref[j, :, pl.ds(0, nstate)], state_ref[j, :, pl.ds(nstate, nstate)]),
            unroll=unroll)
        state_ref[j, :, pl.ds(0, nstate)] = sr
        state_ref[j, :, pl.ds(nstate, nstate)] = si

        y = jnp.dot(bu_ref[...].astype(BF16), wc_ref[j], preferred_element_type=F32)
        y = y + dskip_ref[:, cs] * xn_ref[:, cs]
        g_ref[:, cs] = jax.nn.gelu(y).astype(BF16)

    z = jnp.dot(g_ref[...], wglu_ref[...], preferred_element_type=F32) + bglu_ref[...]
    d = out_ref.shape[-1]
    out_ref[...] = h_ref[...] + z[:, :d] * jax.nn.sigmoid(z[:, d:])


def _s5_discretise(a_re, a_im, log_dt, b_re, b_im, c_re, c_im, group_tile):
    g, p = a_re.shape
    c = b_re.shape[-1]
    lam = lax.complex(a_re, a_im)
    dt = jnp.exp(log_dt)[:, None]
    lam_bar = jnp.exp(lam * dt)
    b_bar = ((lam_bar - 1.0) / lam)[..., None] * lax.complex(b_re, b_im)
    n_tiles = g // group_tile
    eye = jnp.eye(group_tile, dtype=F32)

    def pack_in(m):
        m = m.reshape(n_tiles, group_tile, p, c)
        w = jnp.einsum('jgpc,gh->jgchp', m, eye)
        return w.reshape(n_tiles, group_tile * c, group_tile * p)

    def pack_out(m):
        m = m.reshape(n_tiles, group_tile, c, p)
        w = jnp.einsum('jgcp,gh->jgphc', m, eye)
        return w.reshape(n_tiles, group_tile * p, group_tile * c)

    wb = jnp.concatenate([pack_in(jnp.real(b_bar)), pack_in(jnp.imag(b_bar))], axis=-1)
    wc = jnp.concatenate([pack_out(c_re), pack_out(-c_im)], axis=1)
    lam_t = jnp.stack([jnp.real(lam_bar).reshape(n_tiles, group_tile * p),
                       jnp.imag(lam_bar).reshape(n_tiles, group_tile * p)], axis=1)
    return wb.astype(BF16), wc.astype(BF16), lam_t


def _s5_layer(h, nbatch, a_norm, a_re, a_im, log_dt, b_re, b_im, c_re, c_im, d_skip,
              w_glu, b_glu):
    rows, d = h.shape
    length = rows // nbatch
    g, p = a_re.shape
    c = d // g
    group_tile = min(g, V7X_MXU_DIM // c)
    tile_ch = group_tile * c
    nstate = group_tile * p
    n_tiles = g // group_tile
    tsteps = min(64, length)
    tb = tsteps * nbatch
    wb, wc, lam_t = _s5_discretise(a_re, a_im, log_dt, b_re, b_im, c_re, c_im, group_tile)
    est = (4 * tb * d * 4 + tb * 2 * nstate * 4 + tb * d * 8 + wb.size * 2 + wc.size * 2
           + w_glu.size * 2 + 2 * tb * 2 * d * 4)
    kern = functools.partial(_s5_kernel, n_tiles=n_tiles, tile_ch=tile_ch, nstate=nstate,
                             nbatch=nbatch, tsteps=tsteps, unroll=min(8, tsteps))
    return pl.pallas_call(
        kern,
        grid=(length // tsteps,),
        in_specs=[
            pl.BlockSpec((tb, d), lambda i: (i, 0)),
            _resident((1, d)),
            _resident(wb.shape),
            _resident(lam_t.shape),
            _resident(wc.shape),
            _resident((1, d)),
            _resident(w_glu.shape),
            _resident((1, 2 * d)),
        ],
        out_specs=pl.BlockSpec((tb, d), lambda i: (i, 0)),
        out_shape=jax.ShapeDtypeStruct((rows, d), F32),
        scratch_shapes=[
            pltpu.VMEM((n_tiles, nbatch, 2 * nstate), F32),
            pltpu.VMEM((tb, 2 * nstate), F32),
            pltpu.VMEM((tb, d), F32),
            pltpu.VMEM((tb, d), BF16),
            pltpu.VMEM((tb, d), BF16),
        ],
        compiler_params=_params(("arbitrary",), est),
        name="s5_mixer",
    )(h, a_norm.reshape(1, d), wb, lam_t, wc, d_skip.reshape(1, d),
      w_glu.astype(BF16), b_glu.reshape(1, 2 * d))


def _ffn_kernel(*refs, has_attn, shift, halo, tiles_per_seq, n_fc, tm):
    if has_attn:
        (h_ref, o_ref, wo_ref, fnorm_ref, wup_ref, cw_ref, cb_ref, wdown_ref,
         out_ref, carry_ref, hu_ref, xnb_ref, acc_ref) = refs
    else:
        (h_ref, fnorm_ref, wup_ref, cw_ref, cb_ref, wdown_ref,
         out_ref, carry_ref, hu_ref, xnb_ref, acc_ref) = refs

    @pl.when(pl.program_id(0) % tiles_per_seq == 0)
    def _():
        carry_ref[...] = jnp.zeros_like(carry_ref)

    h1 = h_ref[...]
    if has_attn:
        h1 = h1 + jnp.dot(o_ref[...], wo_ref[...], preferred_element_type=F32)
    out_ref[...] = h1
    xnb_ref[...] = _rmsnorm(h1, fnorm_ref[...]).astype(BF16)
    acc_ref[...] = jnp.zeros_like(acc_ref)

    def conv_branch(idx):
        hu_ref[pl.ds(0, halo), :] = carry_ref[idx]
        hu_ref[pl.ds(halo, tm), :] = jnp.dot(xnb_ref[...], wup_ref[idx],
                                             preferred_element_type=F32)
        carry_ref[idx] = hu_ref[pl.ds(tm, halo), :]
        w = cw_ref[idx]
        return (w[2:3, :] * hu_ref[pl.ds(halo, tm), :]
                + w[1:2, :] * hu_ref[pl.ds(halo - shift, tm), :]
                + w[0:1, :] * hu_ref[pl.ds(halo - 2 * shift, tm), :]
                + cb_ref[idx])

    def chunk(c, carry):
        gate = conv_branch(c)
        val = conv_branch(n_fc + c)
        act = (jax.nn.silu(gate) * val).astype(BF16)
        acc_ref[...] += jnp.dot(act, wdown_ref[c], preferred_element_type=F32)
        return carry

    lax.fori_loop(0, n_fc, chunk, 0)
    out_ref[...] = out_ref[...] + acc_ref[...]


def _ffn_layer(h, h_map, n_rows, d, shift, tiles_per_seq, tm, ffn_norm, w_up, conv_w, conv_b,
               w_down, attn=None):
    f = w_down.shape[0]
    fc = V7X_MXU_DIM
    n_fc = f // fc
    halo = -(-2 * shift // V7X_SUBLANES) * V7X_SUBLANES
    wup = w_up.astype(BF16).reshape(d, 2 * n_fc, fc).transpose(1, 0, 2)
    cw = conv_w.reshape(CONV_WIDTH, 2 * n_fc, fc).transpose(1, 0, 2)
    cb = conv_b.reshape(2 * n_fc, 1, fc)
    wdown = w_down.astype(BF16).reshape(n_fc, fc, d)
    row_map = lambda i: (i, 0)
    in_specs = [pl.BlockSpec((tm, d), h_map)]
    args = [h]
    if attn is not None:
        o, w_o = attn
        in_specs += [pl.BlockSpec((tm, d), row_map), _resident(w_o.shape)]
        args += [o, w_o.astype(BF16)]
    in_specs += [_resident((1, d)), _resident(wup.shape), _resident(cw.shape),
                 _resident(cb.shape), _resident(wdown.shape)]
    args += [ffn_norm.reshape(1, d), wup, cw, cb, wdown]
    est = (wup.size * 2 + wdown.size * 2 + d * d * 2 + 6 * tm * d * 4 + tm * d * 6
           + (tm + halo) * fc * 4 + 2 * n_fc * halo * fc * 4 + 4 * tm * fc * 4)
    kern = functools.partial(_ffn_kernel, has_attn=attn is not None, shift=shift, halo=halo,
                             tiles_per_seq=tiles_per_seq, n_fc=n_fc, tm=tm)
    return pl.pallas_call(
        kern,
        grid=(n_rows // tm,),
        in_specs=in_specs,
        out_specs=pl.BlockSpec((tm, d), row_map),
        out_shape=jax.ShapeDtypeStruct((n_rows, d), F32),
        scratch_shapes=[
            pltpu.VMEM((2 * n_fc, halo, fc), F32),
            pltpu.VMEM((tm + halo, fc), F32),
            pltpu.VMEM((tm, d), BF16),
            pltpu.VMEM((tm, d), F32),
        ],
        compiler_params=_params(("arbitrary",), est),
        name="conv_ffn_attn" if attn is not None else "conv_ffn",
    )(*args)


def _rope(x, cc, ss, dh):
    return x * cc + pltpu.roll(x, dh // 2, axis=1) * ss


def _kv_kernel(h_ref, nrm_ref, wkv_ref, knorm_ref, cc_ref, ss_ref,
               kaug_ref, v_ref, kmean_ref, *, n_heads, dh):
    xn = _rmsnorm(h_ref[...], nrm_ref[...]).astype(BF16)
    kv = jnp.dot(xn, wkv_ref[...], preferred_element_type=F32)
    rows = kv.shape[0]
    blk = pl.program_id(1)
    lane = lax.broadcasted_iota(jnp.int32, (rows, dh), 1)
    onehot = jnp.where(lane == blk, 1.0, 0.0).astype(BF16)
    cc = cc_ref[...]
    ss = ss_ref[...]
    means = []
    for hd in range(n_heads):
        kh = _rmsnorm(kv[:, hd * dh:(hd + 1) * dh], knorm_ref[...])
        kr = _rope(kh, cc, ss, dh)
        kaug_ref[0, hd, :, pl.ds(0, dh)] = kr.astype(BF16)
        kaug_ref[0, hd, :, pl.ds(dh, dh)] = onehot
        v_ref[0, hd] = kv[:, (n_heads + hd) * dh:(n_heads + hd + 1) * dh].astype(BF16)
        means.append(jnp.mean(kr, axis=0, keepdims=True))
    kmean_ref[0, 0] = jnp.concatenate(means, axis=0)


def _kv_proj(h_view, nbatch, length, d, kv_norm, w_kv, k_norm, cc, ss):
    dh = k_norm.shape[0]
    n_heads = w_kv.shape[1] // (2 * dh)
    nblk = length // MOBA_BLOCK
    t = MOBA_BLOCK
    est = w_kv.size * 2 + 2 * t * d * 4 + 3 * t * 2 * n_heads * dh * 4 + 6 * n_heads * t * dh * 2
    return pl.pallas_call(
        functools.partial(_kv_kernel, n_heads=n_heads, dh=dh),
        grid=(nbatch, nblk),
        in_specs=[
            pl.BlockSpec((t, d), lambda b, i: (i, b)),
            _resident((1, d)),
            _resident(w_kv.shape),
            _resident((1, dh)),
            pl.BlockSpec((t, dh), lambda b, i: (i, 0)),
            pl.BlockSpec((t, dh), lambda b, i: (i, 0)),
        ],
        out_specs=[
            pl.BlockSpec((1, n_heads, t, 2 * dh), lambda b, i: (b, 0, i, 0)),
            pl.BlockSpec((1, n_heads, t, dh), lambda b, i: (b, 0, i, 0)),
            pl.BlockSpec((1, 1, n_heads, dh), lambda b, i: (b, i, 0, 0)),
        ],
        out_shape=[
            jax.ShapeDtypeStruct((nbatch, n_heads, length, 2 * dh), BF16),
            jax.ShapeDtypeStruct((nbatch, n_heads, length, dh), BF16),
            jax.ShapeDtypeStruct((nbatch, nblk, n_heads, dh), F32),
        ],
        compiler_params=_params(("parallel", "parallel"), est),
        name="kv_proj",
    )(h_view, kv_norm.reshape(1, d), w_kv.astype(BF16), k_norm.reshape(1, dh), cc, ss)


def _q_kernel(h_ref, nrm_ref, wq_ref, qnorm_ref, cc_ref, ss_ref, kmean_ref, qaug_ref,
              *, n_heads, dh, nblk, topk, scale):
    xn = _rmsnorm(h_ref[...], nrm_ref[...]).astype(BF16)
    q = jnp.dot(xn, wq_ref[...], preferred_element_type=F32)
    rows = q.shape[0]
    cur = pl.program_id(1)
    cc = cc_ref[...]
    ss = ss_ref[...]
    blk_id = lax.broadcasted_iota(jnp.int32, (nblk, rows), 0)
    for hd in range(n_heads):
        qh = _rmsnorm(q[:, hd * dh:(hd + 1) * dh], qnorm_ref[...])
        qr = _rope(qh, cc, ss, dh)
        gate_t = lax.dot_general(kmean_ref[0, hd], qr, (((1,), (1,)), ((), ())),
                                 precision=lax.Precision.HIGHEST,
                                 preferred_element_type=F32)
        gate_t = jnp.where(blk_id < cur, gate_t, -jnp.inf)
        sel_rows = []
        for j in range(nblk):
            gj = gate_t[j:j + 1, :]
            beats = jnp.where((gate_t > gj) | ((gate_t == gj) & (blk_id < j)), 1.0, 0.0)
            rank = jnp.sum(beats, axis=0, keepdims=True)
            sel_rows.append(jnp.where(((rank < topk) & (j < cur)) | (j == cur), 1.0, 0.0))
        sel_t = jnp.concatenate(
            sel_rows + [jnp.zeros((dh - nblk, rows), F32)], axis=0)
        sel = sel_t.T
        qaug_ref[0, hd, :, pl.ds(0, dh)] = (qr * scale).astype(BF16)
        qaug_ref[0, hd, :, pl.ds(dh, dh)] = jnp.where(sel > 0.5, 0.0, MASK_BIAS).astype(BF16)


def _q_proj(h, h_map, nbatch, length, d, b_norm, w_q, q_norm, cc, ss, kmean):
    dh = q_norm.shape[0]
    n_heads = w_q.shape[1] // dh
    nblk = length // MOBA_BLOCK
    t = MOBA_BLOCK
    est = w_q.size * 2 + 2 * t * d * 4 + 3 * t * n_heads * dh * 4 + 4 * n_heads * t * dh * 2
    kern = functools.partial(_q_kernel, n_heads=n_heads, dh=dh, nblk=nblk,
                             topk=min(MOBA_TOPK, nblk), scale=dh ** -0.5)
    return pl.pallas_call(
        kern,
        grid=(nbatch, nblk),
        in_specs=[
            pl.BlockSpec((t, d), h_map),
            _resident((1, d)),
            _resident(w_q.shape),
            _resident((1, dh)),
            pl.BlockSpec((t, dh), lambda b, i: (i, 0)),
            pl.BlockSpec((t, dh), lambda b, i: (i, 0)),
            pl.BlockSpec((1, n_heads, nblk, dh), lambda b, i: (b, 0, 0, 0)),
        ],
        out_specs=pl.BlockSpec((1, n_heads, t, 2 * dh), lambda b, i: (b, 0, i, 0)),
        out_shape=jax.ShapeDtypeStruct((nbatch, n_heads, length, 2 * dh), BF16),
        compiler_params=_params(("parallel", "parallel"), est),
        name="q_proj_gate",
    )(h, b_norm.reshape(1, d), w_q.astype(BF16), q_norm.reshape(1, dh), cc, ss, kmean)


def _attn_kernel(q_ref, k_ref, v_ref, o_ref, s_ref, *, blk, nblk):
    nt = (((1,), (1,)), ((), ()))
    qpos = lax.broadcasted_iota(jnp.int32, (blk, blk), 0)
    kpos = lax.broadcasted_iota(jnp.int32, (blk, blk), 1)
    causal = kpos <= qpos
    for c in range(nblk):
        slot = c % 2
        q = q_ref[0, 0, c * blk:(c + 1) * blk, :]
        m_acc = None
        for j in range(c + 1):
            s = lax.dot_general(q, k_ref[0, 0, j * blk:(j + 1) * blk, :], nt,
                                preferred_element_type=F32)
            if j == c:
                s = jnp.where(causal, s, -jnp.inf)
            s_ref[slot, :, j * blk:(j + 1) * blk] = s
            m_acc = s if m_acc is None else jnp.maximum(m_acc, s)
        m = jnp.max(m_acc, axis=1, keepdims=True)
        l_acc = None
        acc = None
        for j in range(c + 1):
            p = jnp.exp(s_ref[slot, :, j * blk:(j + 1) * blk] - m)
            pv = jnp.dot(p.astype(BF16), v_ref[0, 0, j * blk:(j + 1) * blk, :],
                         preferred_element_type=F32)
            l_acc = p if l_acc is None else l_acc + p
            acc = pv if acc is None else acc + pv
        l = jnp.sum(l_acc, axis=1, keepdims=True)
        o_ref[c * blk:(c + 1) * blk, :] = (acc / l).astype(o_ref.dtype)


def _attention(qaug, kaug, v):
    nbatch, n_heads, length, dh2 = qaug.shape
    dh = dh2 // 2
    blk = MOBA_BLOCK
    nblk = length // blk
    est = 4 * length * dh2 * 2 + 4 * length * dh * 2 + 2 * blk * length * 4
    return pl.pallas_call(
        functools.partial(_attn_kernel, blk=blk, nblk=nblk),
        grid=(nbatch, n_heads),
        in_specs=[
            pl.BlockSpec((1, 1, length, dh2), lambda b, h: (b, h, 0, 0)),
            pl.BlockSpec((1, 1, length, dh2), lambda b, h: (b, h, 0, 0)),
            pl.BlockSpec((1, 1, length, dh), lambda b, h: (b, h, 0, 0)),
        ],
        out_specs=pl.BlockSpec((length, dh), lambda b, h: (b, h)),
        out_shape=jax.ShapeDtypeStruct((nbatch * length, n_heads * dh), BF16),
        scratch_shapes=[pltpu.VMEM((2, blk, length), F32)],
        compiler_params=_params(("parallel", "parallel"), est),
        name="moba_attention",
    )(qaug, kaug, v)


def _rope_tables(length, dh):
    half = dh // 2
    inv_freq = ROPE_THETA ** (-jnp.arange(half, dtype=F32) * 2.0 / dh)
    ang = jnp.arange(length, dtype=F32)[:, None] * inv_freq[None, :]
    cos, sin = jnp.cos(ang), jnp.sin(ang)
    return jnp.concatenate([cos, cos], axis=-1), jnp.concatenate([-sin, sin], axis=-1)


def kernel(x, a_norm, s5_A_re, s5_A_im, s5_log_dt, s5_B_re, s5_B_im, s5_C_re, s5_C_im, s5_D,
           w_glu, b_glu, kv_norm, w_kv, k_norm, b_norm, w_q, q_norm, w_o, ffn_norm, w_up,
           conv_w, conv_b, w_down):
    nbatch, length, d = x.shape
    n_a = a_norm.shape[0]
    n_b = b_norm.shape[0]
    rows = nbatch * length
    assert nbatch == V7X_SUBLANES, "the S5 recurrence keeps the batch on the sublane axis"
    assert length % MOBA_BLOCK == 0 and n_a >= 1 and n_b >= 1
    dh = k_norm.shape[0]
    cc, ss = _rope_tables(length, dh)

    h = x.transpose(1, 0, 2).reshape(rows, d)
    tm_a = min(512, rows)
    for i in range(n_a):
        h = _s5_layer(h, nbatch, a_norm[i], s5_A_re[i], s5_A_im[i], s5_log_dt[i], s5_B_re[i],
                      s5_B_im[i], s5_C_re[i], s5_C_im[i], s5_D[i], w_glu[i], b_glu[i])
        h = _ffn_layer(h, lambda n: (n, 0), rows, d, nbatch, rows // tm_a, tm_a,
                       ffn_norm[i], w_up[i], conv_w[i], conv_b[i], w_down[i])

    h_view = h.reshape(length, nbatch * d)
    kaug, v, kmean = _kv_proj(h_view, nbatch, length, d, kv_norm, w_kv, k_norm, cc, ss)
    kmean = kmean.transpose(0, 2, 1, 3)
    tm_b = min(512, length)
    tps = length // tm_b
    for j in range(n_b):
        layer = n_a + j
        if j == 0:
            src, q_map, f_map = h_view, (lambda b, i: (i, b)), (lambda n: (n % tps, n // tps))
        else:
            nblk = length // MOBA_BLOCK
            src, q_map, f_map = h, (lambda b, i: (b * nblk + i, 0)), (lambda n: (n, 0))
        qaug = _q_proj(src, q_map, nbatch, length, d, b_norm[j], w_q[j], q_norm[j], cc, ss, kmean)
        o = _attention(qaug, kaug, v)
        h = _ffn_layer(src, f_map, rows, d, 1, tps, tm_b, ffn_norm[layer], w_up[layer],
                       conv_w[layer], conv_b[layer], w_down[layer], attn=(o, w_o[j]))
    return h.reshape(nbatch, length, d)
```

```python
import functools

import jax
import jax.numpy as jnp
from jax import lax
from jax.experimental import pallas as pl
from jax.experimental.pallas import tpu as pltpu

MOBA_BLOCK = 256
MOBA_TOPK = 3
CONV_WIDTH = 3
ROPE_THETA = 10000.0
EPS = 1e-6

V7X_SUBLANES = 8
V7X_MXU_DIM = 256
V7X_VMEM_BYTES = 64 * 1024 * 1024

MASK_BIAS = -1e30

F32 = jnp.float32
BF16 = jnp.bfloat16


def _vmem_limit(est_bytes):
    return int(min(V7X_VMEM_BYTES - 6 * 1024 * 1024, max(32 * 1024 * 1024, 2 * est_bytes)))


def _params(semantics, est_bytes):
    return pltpu.CompilerParams(dimension_semantics=semantics,
                                vmem_limit_bytes=_vmem_limit(est_bytes))


def _resident(shape):
    nd = len(shape)
    return pl.BlockSpec(shape, lambda *_: (0,) * nd, pipeline_mode=pl.Buffered(1))


def _layer_resident(arr, layer):
    nd = arr.ndim - 1
    return pl.BlockSpec((None,) + arr.shape[1:], lambda *_: (layer,) + (0,) * nd,
                        pipeline_mode=pl.Buffered(1))


def _rmsnorm(x, g):
    return x * lax.rsqrt(jnp.mean(x * x, axis=-1, keepdims=True) + EPS) * g


def _s5_kernel(h_ref, anorm_ref, wb_ref, lam_ref, wc_ref, dskip_ref, wglu_ref, bglu_ref,
               out_ref, state_ref, bu_ref, xn_ref, xnb_ref, g_ref,
               *, n_tiles, tile_ch, nstate, nbatch, tsteps):
    @pl.when(pl.program_id(0) == 0)
    def _():
        state_ref[...] = jnp.zeros_like(state_ref)

    xn = _rmsnorm(h_ref[...], anorm_ref[...])
    xn_ref[...] = xn
    xnb_ref[...] = xn.astype(BF16)

    for j in range(n_tiles):
        cs = pl.ds(j * tile_ch, tile_ch)
        bu = bu_ref.at[j % 2]
        bu[...] = jnp.dot(xnb_ref[:, cs], wb_ref[j], preferred_element_type=F32)
        lr = jnp.broadcast_to(lam_ref[j, 0:1, :], (nbatch, nstate))
        li = jnp.broadcast_to(lam_ref[j, 1:2, :], (nbatch, nstate))

        sr = state_ref[j, :, pl.ds(0, nstate)]
        si = state_ref[j, :, pl.ds(nstate, nstate)]
        for t in range(tsteps):
            rows = pl.ds(t * nbatch, nbatch)
            nr = lr * sr - li * si + bu[rows, pl.ds(0, nstate)]
            ni = lr * si + li * sr + bu[rows, pl.ds(nstate, nstate)]
            bu[rows, pl.ds(0, nstate)] = nr
            bu[rows, pl.ds(nstate, nstate)] = ni
            sr, si = nr, ni
        state_ref[j, :, pl.ds(0, nstate)] = sr
        state_ref[j, :, pl.ds(nstate, nstate)] = si

        y = jnp.dot(bu[...].astype(BF16), wc_ref[j], preferred_element_type=F32)
        y = y + dskip_ref[:, cs] * xn_ref[:, cs]
        g_ref[:, cs] = jax.nn.gelu(y).astype(BF16)

    z = jnp.dot(g_ref[...], wglu_ref[...], preferred_element_type=F32) + bglu_ref[...]
    d = out_ref.shape[-1]
    out_ref[...] = h_ref[...] + z[:, :d] * jax.nn.sigmoid(z[:, d:])


def _s5_discretise(a_re, a_im, log_dt, b_re, b_im, c_re, c_im, group_tile):
    na, g, p = a_re.shape
    c = b_re.shape[-1]
    lam = lax.complex(a_re, a_im)
    dt = jnp.exp(log_dt)[..., None]
    lam_bar = jnp.exp(lam * dt)
    b_bar = ((lam_bar - 1.0) / lam)[..., None] * lax.complex(b_re, b_im)
    n_tiles = g // group_tile
    eye = jnp.eye(group_tile, dtype=F32)
    bb = jnp.stack([jnp.real(b_bar), jnp.imag(b_bar)], axis=2)
    bb = bb.reshape(na, n_tiles, group_tile, 2, p, c)
    wb = jnp.einsum('ljgrpc,gh->ljgcrhp', bb, eye)
    wb = wb.reshape(na, n_tiles, group_tile * c, 2 * group_tile * p)
    cm = jnp.stack([c_re, -c_im], axis=2).reshape(na, n_tiles, group_tile, 2, c, p)
    wc = jnp.einsum('ljgrcp,gh->ljrgphc', cm, eye)
    wc = wc.reshape(na, n_tiles, 2 * group_tile * p, group_tile * c)
    lam_t = jnp.stack([jnp.real(lam_bar).reshape(na, n_tiles, group_tile * p),
                       jnp.imag(lam_bar).reshape(na, n_tiles, group_tile * p)], axis=2)
    return wb.astype(BF16), wc.astype(BF16), lam_t


def _s5_layer(h, nbatch, layer, a_norm, wb, lam_t, wc, d_skip, wglu, b_glu):
    rows, d = h.shape
    length = rows // nbatch
    n_tiles = wb.shape[1]
    tile_ch = wb.shape[2]
    nstate = wb.shape[3] // 2
    tsteps = min(64, length)
    tb = tsteps * nbatch
    est = (4 * tb * d * 4 + 2 * tb * 2 * nstate * 4 + tb * d * 8
           + 2 * n_tiles * tile_ch * 2 * nstate * 2 + 2 * d * d * 2 + 2 * tb * 2 * d * 4)
    kern = functools.partial(_s5_kernel, n_tiles=n_tiles, tile_ch=tile_ch, nstate=nstate,
                             nbatch=nbatch, tsteps=tsteps)
    return pl.pallas_call(
        kern,
        grid=(length // tsteps,),
        in_specs=[
            pl.BlockSpec((tb, d), lambda i: (i, 0)),
            _layer_resident(a_norm, layer),
            _layer_resident(wb, layer),
            _layer_resident(lam_t, layer),
            _layer_resident(wc, layer),
            _layer_resident(d_skip, layer),
            _layer_resident(wglu, layer),
            _layer_resident(b_glu, layer),
        ],
        out_specs=pl.BlockSpec((tb, d), lambda i: (i, 0)),
        out_shape=jax.ShapeDtypeStruct((rows, d), F32),
        scratch_shapes=[
            pltpu.VMEM((n_tiles, nbatch, 2 * nstate), F32),
            pltpu.VMEM((2, tb, 2 * nstate), F32),
            pltpu.VMEM((tb, d), F32),
            pltpu.VMEM((tb, d), BF16),
            pltpu.VMEM((tb, d), BF16),
        ],
        compiler_params=_params(("arbitrary",), est),
        name="s5_mixer",
    )(h, a_norm, wb, lam_t, wc, d_skip, wglu, b_glu)


def _ffn_kernel(*refs, has_attn, shift, halo, tiles_per_seq, n_fc, fc, tm, n_slots):
    if has_attn:
        (h_ref, o_ref, wo_ref, fnorm_ref, wup_ref, cw_ref, cb_ref, wdown_ref,
         out_ref, carry_ref, xnb_ref, act_ref, *hu) = refs
    else:
        (h_ref, fnorm_ref, wup_ref, cw_ref, cb_ref, wdown_ref,
         out_ref, carry_ref, xnb_ref, act_ref, *hu) = refs
    aligned = shift % V7X_SUBLANES == 0

    @pl.when(pl.program_id(0) % tiles_per_seq == 0)
    def _():
        carry_ref[...] = jnp.zeros_like(carry_ref)

    h1 = h_ref[...]
    if has_attn:
        h1 = h1 + jnp.dot(o_ref[...], wo_ref[...], preferred_element_type=F32)
    out_ref[...] = h1
    xnb_ref[...] = _rmsnorm(h1, fnorm_ref[...]).astype(BF16)

    def conv_branch(idx, slot):
        cols = slice(idx * fc, (idx + 1) * fc)
        u = jnp.dot(xnb_ref[...], wup_ref[:, cols], preferred_element_type=F32)
        w = cw_ref[:, cols]
        if aligned:
            prev = carry_ref[idx]
            carry_ref[idx] = u[tm - halo:, :]
            u1 = jnp.concatenate([prev[halo - shift:, :], u[:tm - shift, :]], axis=0)
            u2 = jnp.concatenate([prev[halo - 2 * shift:, :], u[:tm - 2 * shift, :]], axis=0)
        else:
            hu_ref = hu[0]
            hu_ref[slot, pl.ds(0, halo), :] = carry_ref[idx]
            hu_ref[slot, pl.ds(halo, tm), :] = u
            carry_ref[idx] = u[tm - halo:, :]
            u1 = hu_ref[slot, pl.ds(halo - shift, tm), :]
            u2 = hu_ref[slot, pl.ds(halo - 2 * shift, tm), :]
        return w[2:3, :] * u + w[1:2, :] * u1 + w[0:1, :] * u2 + cb_ref[:, cols]

    for c in range(n_fc):
        gate = conv_branch(c, (2 * c) % n_slots)
        val = conv_branch(n_fc + c, (2 * c + 1) % n_slots)
        act_ref[:, c * fc:(c + 1) * fc] = (jax.nn.silu(gate) * val).astype(BF16)

    out_ref[...] = out_ref[...] + jnp.dot(act_ref[...], wdown_ref[...],
                                          preferred_element_type=F32)


def _ffn_layer(h, h_map, n_rows, d, shift, tiles_per_seq, tm, layer, ffn_norm, wup, conv_w,
               conv_b, wdown, attn=None):
    f = wdown.shape[1]
    fc = V7X_MXU_DIM
    n_fc = f // fc
    halo = -(-2 * shift // V7X_SUBLANES) * V7X_SUBLANES
    aligned = shift % V7X_SUBLANES == 0
    n_slots = 4
    row_map = lambda i: (i, 0)
    in_specs = [pl.BlockSpec((tm, d), h_map)]
    args = [h]
    if attn is not None:
        o, wo, j = attn
        in_specs += [pl.BlockSpec((tm, d), row_map), _layer_resident(wo, j)]
        args += [o, wo]
    in_specs += [_layer_resident(ffn_norm, layer), _layer_resident(wup, layer),
                 _layer_resident(conv_w, layer), _layer_resident(conv_b, layer),
                 _layer_resident(wdown, layer)]
    args += [ffn_norm, wup, conv_w, conv_b, wdown]
    scratch = [
        pltpu.VMEM((2 * n_fc, halo, fc), F32),
        pltpu.VMEM((tm, d), BF16),
        pltpu.VMEM((tm, f), BF16),
    ]
    if not aligned:
        scratch.append(pltpu.VMEM((n_slots, tm + halo, fc), F32))
    est = (d * 2 * f * 2 + f * d * 2 + d * d * 2 + 6 * tm * d * 4 + tm * d * 2 + tm * f * 2
           + n_slots * (tm + halo) * fc * 4 + 2 * n_fc * halo * fc * 4 + 8 * tm * fc * 4)
    kern = functools.partial(_ffn_kernel, has_attn=attn is not None, shift=shift, halo=halo,
                             tiles_per_seq=tiles_per_seq, n_fc=n_fc, fc=fc, tm=tm,
                             n_slots=n_slots)
    return pl.pallas_call(
        kern,
        grid=(n_rows // tm,),
        in_specs=in_specs,
        out_specs=pl.BlockSpec((tm, d), row_map),
        out_shape=jax.ShapeDtypeStruct((n_rows, d), F32),
        scratch_shapes=scratch,
        compiler_params=_params(("arbitrary",), est),
        name="conv_ffn_attn" if attn is not None else "conv_ffn",
    )(*args)


def _rope(x, cc, ss, dh):
    return x * cc + pltpu.roll(x, dh // 2, axis=1) * ss


def _kv_kernel(h_ref, nrm_ref, wkv_ref, knorm_ref, cc_ref, ss_ref,
               kaug_ref, v_ref, kmean_ref, *, n_heads, dh):
    xn = _rmsnorm(h_ref[...], nrm_ref[...]).astype(BF16)
    kv = jnp.dot(xn, wkv_ref[...], preferred_element_type=F32)
    rows = kv.shape[0]
    blk = pl.program_id(1)
    lane = lax.broadcasted_iota(jnp.int32, (rows, dh), 1)
    onehot = jnp.where(lane == blk, 1.0, 0.0).astype(BF16)
    cc = cc_ref[...]
    ss = ss_ref[...]
    means = []
    for hd in range(n_heads):
        kh = _rmsnorm(kv[:, hd * dh:(hd + 1) * dh], knorm_ref[...])
        kr = _rope(kh, cc, ss, dh)
        kaug_ref[0, hd, :, pl.ds(0, dh)] = kr.astype(BF16)
        kaug_ref[0, hd, :, pl.ds(dh, dh)] = onehot
        v_ref[0, hd] = kv[:, (n_heads + hd) * dh:(n_heads + hd + 1) * dh].astype(BF16)
        means.append(jnp.mean(kr, axis=0, keepdims=True))
    kmean_ref[0, 0] = jnp.concatenate(means, axis=0)


def _kv_proj(h_view, nbatch, length, d, kv_norm, w_kv, k_norm, cc, ss):
    dh = k_norm.shape[0]
    n_heads = w_kv.shape[1] // (2 * dh)
    nblk = length // MOBA_BLOCK
    t = MOBA_BLOCK
    est = w_kv.size * 2 + 2 * t * d * 4 + 3 * t * 2 * n_heads * dh * 4 + 6 * n_heads * t * dh * 2
    return pl.pallas_call(
        functools.partial(_kv_kernel, n_heads=n_heads, dh=dh),
        grid=(nbatch, nblk),
        in_specs=[
            pl.BlockSpec((t, d), lambda b, i: (i, b)),
            _resident((1, d)),
            _resident(w_kv.shape),
            _resident((1, dh)),
            pl.BlockSpec((t, dh), lambda b, i: (i, 0)),
            pl.BlockSpec((t, dh), lambda b, i: (i, 0)),
        ],
        out_specs=[
            pl.BlockSpec((1, n_heads, t, 2 * dh), lambda b, i: (b, 0, i, 0)),
            pl.BlockSpec((1, n_heads, t, dh), lambda b, i: (b, 0, i, 0)),
            pl.BlockSpec((1, 1, n_heads, dh), lambda b, i: (b, i, 0, 0)),
        ],
        out_shape=[
            jax.ShapeDtypeStruct((nbatch, n_heads, length, 2 * dh), BF16),
            jax.ShapeDtypeStruct((nbatch, n_heads, length, dh), BF16),
            jax.ShapeDtypeStruct((nbatch, nblk, n_heads, dh), F32),
        ],
        compiler_params=_params(("parallel", "parallel"), est),
        name="kv_proj",
    )(h_view, kv_norm.reshape(1, d), w_kv.astype(BF16), k_norm.reshape(1, dh), cc, ss)


def _q_kernel(h_ref, nrm_ref, wq_ref, qnorm_ref, cc_ref, ss_ref, kmean_ref, qaug_ref,
              *, n_heads, dh, nblk, topk, scale):
    xn = _rmsnorm(h_ref[...], nrm_ref[...]).astype(BF16)
    q = jnp.dot(xn, wq_ref[...], preferred_element_type=F32)
    rows = q.shape[0]
    cur = pl.program_id(1)
    cc = cc_ref[...]
    ss = ss_ref[...]
    blk_id = lax.broadcasted_iota(jnp.int32, (nblk, rows), 0)
    for hd in range(n_heads):
        qh = _rmsnorm(q[:, hd * dh:(hd + 1) * dh], qnorm_ref[...])
        qr = _rope(qh, cc, ss, dh)
        gate_t = lax.dot_general(kmean_ref[0, hd], qr, (((1,), (1,)), ((), ())),
                                 precision=lax.Precision.HIGHEST,
                                 preferred_element_type=F32)
        gate_t = jnp.where(blk_id < cur, gate_t, -jnp.inf)
        sel_rows = []
        for j in range(nblk):
            gj = gate_t[j:j + 1, :]
            beats = jnp.where((gate_t > gj) | ((gate_t == gj) & (blk_id < j)), 1.0, 0.0)
            rank = jnp.sum(beats, axis=0, keepdims=True)
            sel_rows.append(jnp.where(((rank < topk) & (j < cur)) | (j == cur), 1.0, 0.0))
        sel_t = jnp.concatenate(
            sel_rows + [jnp.zeros((dh - nblk, rows), F32)], axis=0)
        sel = sel_t.T
        qaug_ref[0, hd, :, pl.ds(0, dh)] = (qr * scale).astype(BF16)
        qaug_ref[0, hd, :, pl.ds(dh, dh)] = jnp.where(sel > 0.5, 0.0, MASK_BIAS).astype(BF16)


def _q_proj(h, h_map, nbatch, length, d, b_norm, w_q, q_norm, cc, ss, kmean):
    dh = q_norm.shape[0]
    n_heads = w_q.shape[1] // dh
    nblk = length // MOBA_BLOCK
    t = MOBA_BLOCK
    est = w_q.size * 2 + 2 * t * d * 4 + 3 * t * n_heads * dh * 4 + 4 * n_heads * t * dh * 2
    kern = functools.partial(_q_kernel, n_heads=n_heads, dh=dh, nblk=nblk,
                             topk=min(MOBA_TOPK, nblk), scale=dh ** -0.5)
    return pl.pallas_call(
        kern,
        grid=(nbatch, nblk),
        in_specs=[
            pl.BlockSpec((t, d), h_map),
            _resident((1, d)),
            _resident(w_q.shape),
            _resident((1, dh)),
            pl.BlockSpec((t, dh), lambda b, i: (i, 0)),
            pl.BlockSpec((t, dh), lambda b, i: (i, 0)),
            pl.BlockSpec((1, n_heads, nblk, dh), lambda b, i: (b, 0, 0, 0)),
        ],
        out_specs=pl.BlockSpec((1, n_heads, t, 2 * dh), lambda b, i: (b, 0, i, 0)),
        out_shape=jax.ShapeDtypeStruct((nbatch, n_heads, length, 2 * dh), BF16),
        compiler_params=_params(("parallel", "parallel"), est),
        name="q_proj_gate",
    )(h, b_norm.reshape(1, d), w_q.astype(BF16), q_norm.reshape(1, dh), cc, ss, kmean)


def _attn_kernel(q_ref, k_ref, v_ref, o_ref, s_ref, *, blk, nblk):
    nt = (((1,), (1,)), ((), ()))
    qpos = lax.broadcasted_iota(jnp.int32, (blk, blk), 0)
    kpos = lax.broadcasted_iota(jnp.int32, (blk, blk), 1)
    causal = kpos <= qpos
    for c in range(nblk):
        slot = c % 2
        q = q_ref[0, 0, c * blk:(c + 1) * blk, :]
        m_acc = None
        for j in range(c + 1):
            s = lax.dot_general(q, k_ref[0, 0, j * blk:(j + 1) * blk, :], nt,
                                preferred_element_type=F32)
            if j == c:
                s = jnp.where(causal, s, -jnp.inf)
            s_ref[slot, :, j * blk:(j + 1) * blk] = s
            m_acc = s if m_acc is None else jnp.maximum(m_acc, s)
        m = jnp.max(m_acc, axis=1, keepdims=True)
        l_acc = None
        acc = None
        for j in range(c + 1):
            p = jnp.exp(s_ref[slot, :, j * blk:(j + 1) * blk] - m)
            pv = jnp.dot(p.astype(BF16), v_ref[0, 0, j * blk:(j + 1) * blk, :],
                         preferred_element_type=F32)
            l_acc = p if l_acc is None else l_acc + p
            acc = pv if acc is None else acc + pv
        l = jnp.sum(l_acc, axis=1, keepdims=True)
        o_ref[c * blk:(c + 1) * blk, :] = (acc / l).astype(o_ref.dtype)


def _attention(qaug, kaug, v):
    nbatch, n_heads, length, dh2 = qaug.shape
    dh = dh2 // 2
    blk = MOBA_BLOCK
    nblk = length // blk
    est = 4 * length * dh2 * 2 + 4 * length * dh * 2 + 2 * blk * length * 4
    return pl.pallas_call(
        functools.partial(_attn_kernel, blk=blk, nblk=nblk),
        grid=(nbatch, n_heads),
        in_specs=[
            pl.BlockSpec((1, 1, length, dh2), lambda b, h: (b, h, 0, 0)),
            pl.BlockSpec((1, 1, length, dh2), lambda b, h: (b, h, 0, 0)),
            pl.BlockSpec((1, 1, length, dh), lambda b, h: (b, h, 0, 0)),
        ],
        out_specs=pl.BlockSpec((length, dh), lambda b, h: (b, h)),
        out_shape=jax.ShapeDtypeStruct((nbatch * length, n_heads * dh), BF16),
        scratch_shapes=[pltpu.VMEM((2, blk, length), F32)],
        compiler_params=_params(("parallel", "parallel"), est),
        name="moba_attention",
    )(qaug, kaug, v)


def _rope_tables(length, dh):
    half = dh // 2
    inv_freq = ROPE_THETA ** (-jnp.arange(half, dtype=F32) * 2.0 / dh)
    ang = jnp.arange(length, dtype=F32)[:, None] * inv_freq[None, :]
    cos, sin = jnp.cos(ang), jnp.sin(ang)
    return jnp.concatenate([cos, cos], axis=-1), jnp.concatenate([-sin, sin], axis=-1)


def kernel(x, a_norm, s5_A_re, s5_A_im, s5_log_dt, s5_B_re, s5_B_im, s5_C_re, s5_C_im, s5_D,
           w_glu, b_glu, kv_norm, w_kv, k_norm, b_norm, w_q, q_norm, w_o, ffn_norm, w_up,
           conv_w, conv_b, w_down):
    nbatch, length, d = x.shape
    n_a = a_norm.shape[0]
    n_b = b_norm.shape[0]
    rows = nbatch * length
    assert nbatch == V7X_SUBLANES, "the S5 recurrence keeps the batch on the sublane axis"
    assert length % MOBA_BLOCK == 0 and n_a >= 1 and n_b >= 1
    dh = k_norm.shape[0]
    cc, ss = _rope_tables(length, dh)

    depth = n_a + n_b
    g = s5_A_re.shape[1]
    group_tile = min(g, V7X_MXU_DIM // (d // g))
    wb, wc, lam_t = _s5_discretise(s5_A_re, s5_A_im, s5_log_dt, s5_B_re, s5_B_im, s5_C_re,
                                   s5_C_im, group_tile)
    a_norm3, d_skip3 = a_norm.reshape(n_a, 1, d), s5_D.reshape(n_a, 1, d)
    wglu, b_glu3 = w_glu.astype(BF16), b_glu.reshape(n_a, 1, 2 * d)
    fnorm3 = ffn_norm.reshape(depth, 1, d)
    wup, wdown = w_up.astype(BF16), w_down.astype(BF16)
    conv_b3 = conv_b.reshape(depth, 1, conv_b.shape[-1])
    wo = w_o.astype(BF16)

    h = x.transpose(1, 0, 2).reshape(rows, d)
    tm_a = min(512, rows)
    for i in range(n_a):
        h = _s5_layer(h, nbatch, i, a_norm3, wb, lam_t, wc, d_skip3, wglu, b_glu3)
        h = _ffn_layer(h, lambda n: (n, 0), rows, d, nbatch, rows // tm_a, tm_a, i,
                       fnorm3, wup, conv_w, conv_b3, wdown)

    h_view = h.reshape(length, nbatch * d)
    kaug, v, kmean = _kv_proj(h_view, nbatch, length, d, kv_norm, w_kv, k_norm, cc, ss)
    kmean = kmean.transpose(0, 2, 1, 3)
    tm_b = min(512, length)
    tps = length // tm_b
    nblk = length // MOBA_BLOCK
    for j in range(n_b):
        if j == 0:
            src, q_map, f_map = h_view, (lambda b, i: (i, b)), (lambda n: (n % tps, n // tps))
        else:
            src, q_map, f_map = h, (lambda b, i: (b * nblk + i, 0)), (lambda n: (n, 0))
        qaug = _q_proj(src, q_map, nbatch, length, d, b_norm[j], w_q[j], q_norm[j], cc, ss, kmean)
        o = _attention(qaug, kaug, v)
        h = _ffn_layer(src, f_map, rows, d, 1, tps, tm_b, n_a + j, fnorm3, wup, conv_w,
                       conv_b3, wdown, attn=(o, wo, j))
    return h.reshape(nbatch, length, d)
```

```python
import functools

import jax
import jax.numpy as jnp
from jax import lax
from jax.experimental import pallas as pl
from jax.experimental.pallas import tpu as pltpu

MOBA_BLOCK = 256
MOBA_TOPK = 3
CONV_WIDTH = 3
ROPE_THETA = 10000.0
EPS = 1e-6

V7X_SUBLANES = 8
V7X_LANES = 128
V7X_MXU_DIM = 256
V7X_VMEM_BYTES = 64 * 1024 * 1024

MASK_BIAS = -1e30

F32 = jnp.float32
BF16 = jnp.bfloat16


def _vmem_limit(est_bytes):
    return int(min(V7X_VMEM_BYTES - 6 * 1024 * 1024, max(32 * 1024 * 1024, 2 * est_bytes)))


def _params(semantics, est_bytes):
    return pltpu.CompilerParams(dimension_semantics=semantics,
                                vmem_limit_bytes=_vmem_limit(est_bytes))


def _resident(shape):
    nd = len(shape)
    return pl.BlockSpec(shape, lambda *_: (0,) * nd, pipeline_mode=pl.Buffered(1))


def _layer_resident(arr, layer):
    nd = arr.ndim - 1
    return pl.BlockSpec((None,) + arr.shape[1:], lambda *_: (layer,) + (0,) * nd,
                        pipeline_mode=pl.Buffered(1))


def _rmsnorm(x, g):
    return x * lax.rsqrt(jnp.mean(x * x, axis=-1, keepdims=True) + EPS) * g


def _s5_kernel(*refs, n_slabs, slab_ch, nstate, nbatch, npairs, batch_major_in):
    n_in = n_slabs * slab_ch // V7X_LANES if batch_major_in else 1
    h_refs = refs[:n_in]
    (anorm_ref, w2_ref, lam2_ref, w3_ref, w1_ref, dskip_ref, wglu_ref, bglu_ref,
     out_ref, state_ref, z_ref, x_ref, xn_ref, pair_ref, g_ref, *xt) = refs[n_in:]
    half = npairs * nbatch
    d = x_ref.shape[-1]

    @pl.when(pl.program_id(0) == 0)
    def _():
        state_ref[...] = jnp.zeros_like(state_ref)

    if batch_major_in:
        xt_ref = xt[0]
        for s in range(n_in):
            for b in range(nbatch):
                for par in range(2):
                    xt_ref[s, pl.ds(par * half + b, npairs, stride=nbatch), :] = (
                        h_refs[s][b, pl.ds(par, npairs, stride=2), :])
            x_ref[:, s * V7X_LANES:(s + 1) * V7X_LANES] = xt_ref[s]
    else:
        h_ref = h_refs[0]
        x_ref[pl.ds(0, half), :] = h_ref[:, 0].reshape(half, d)
        x_ref[pl.ds(half, half), :] = h_ref[:, 1].reshape(half, d)

    xn = _rmsnorm(x_ref[...], anorm_ref[...])
    xn_ref[...] = xn
    xnb = xn.astype(BF16)

    for s in range(n_slabs):
        cs = slice(s * slab_ch, (s + 1) * slab_ch)
        pair_ref[s] = jnp.concatenate([xnb[:half, cs], xnb[half:, cs]], axis=1)
        zb = z_ref.at[s % 2]
        zb[...] = jnp.dot(pair_ref[s], w2_ref[s], preferred_element_type=F32)
        lr = jnp.broadcast_to(lam2_ref[s, 0:1, :], (nbatch, nstate))
        li = jnp.broadcast_to(lam2_ref[s, 1:2, :], (nbatch, nstate))
        sr = state_ref[s, :, pl.ds(0, nstate)]
        si = state_ref[s, :, pl.ds(nstate, nstate)]
        for k in range(npairs):
            rows = pl.ds(k * nbatch, nbatch)
            zr = zb[rows, pl.ds(0, nstate)]
            zi = zb[rows, pl.ds(nstate, nstate)]
            zb[rows, pl.ds(0, nstate)] = sr
            zb[rows, pl.ds(nstate, nstate)] = si
            sr, si = lr * sr - li * si + zr, lr * si + li * sr + zi
        state_ref[s, :, pl.ds(0, nstate)] = sr
        state_ref[s, :, pl.ds(nstate, nstate)] = si

        y = (jnp.dot(zb[...].astype(BF16), w3_ref[s], preferred_element_type=F32)
             + jnp.dot(pair_ref[s], w1_ref[s], preferred_element_type=F32))
        dsk = dskip_ref[:, cs]
        g_ref[pl.ds(0, half), cs] = jax.nn.gelu(
            y[:, :slab_ch] + dsk * xn_ref[pl.ds(0, half), cs]).astype(BF16)
        g_ref[pl.ds(half, half), cs] = jax.nn.gelu(
            y[:, slab_ch:] + dsk * xn_ref[pl.ds(half, half), cs]).astype(BF16)

    z = jnp.dot(g_ref[...], wglu_ref[...], preferred_element_type=F32) + bglu_ref[...]
    res = x_ref[...] + z[:, :d] * jax.nn.sigmoid(z[:, d:])
    out_ref[:, 0] = res[:half].reshape(npairs, nbatch, d)
    out_ref[:, 1] = res[half:].reshape(npairs, nbatch, d)


def _block_diag_lanes(src, n_blocks, row_block, col_block):
    rows = src.shape[-3]
    tiled = jnp.tile(src, (1,) * (src.ndim - 1) + (n_blocks,))
    g = (lax.broadcasted_iota(jnp.int32, (rows, n_blocks * col_block), 0) // row_block) % n_blocks
    h = lax.broadcasted_iota(jnp.int32, (rows, n_blocks * col_block), 1) // col_block
    out = jnp.where((g == h)[:, None, :], tiled, 0.0)
    return out.reshape(src.shape[:-2] + (src.shape[-2] * n_blocks * col_block,))


def _s5_prepare(a_re, a_im, log_dt, b_re, b_im, c_re, c_im, gs):
    na, g, p = a_re.shape
    c = b_re.shape[-1]
    ns = g // gs
    hi = lax.Precision.HIGHEST
    lam = lax.complex(a_re, a_im)
    dt = jnp.exp(log_dt)[..., None]
    lam_bar = jnp.exp(lam * dt)
    b_bar = ((lam_bar - 1.0) / lam)[..., None] * lax.complex(b_re, b_im)
    c_mat = lax.complex(c_re, c_im)
    lb = lam_bar[..., None] * b_bar
    cl1 = c_mat * lam_bar[:, :, None, :]
    cl2 = cl1 * lam_bar[:, :, None, :]
    lam2 = lam_bar * lam_bar

    def re_prod(cm, bm):
        return (jnp.einsum('lgdp,lgpc->lgdc', jnp.real(cm), jnp.real(bm), precision=hi)
                - jnp.einsum('lgdp,lgpc->lgdc', jnp.imag(cm), jnp.imag(bm), precision=hi))

    rcb = re_prod(c_mat, b_bar)
    rclb = re_prod(cl1, b_bar)

    s2 = jnp.stack([lb, b_bar], axis=1).reshape(na, 2, ns, gs, p, c)
    s2 = s2.transpose(0, 2, 1, 3, 5, 4)
    s2 = jnp.stack([jnp.real(s2), jnp.imag(s2)], axis=-2)
    w2 = _block_diag_lanes(s2.reshape(na, ns, 2 * gs * c, 2, p), gs, c, p)
    s3 = jnp.stack([cl1, cl2], axis=1).reshape(na, 2, ns, gs, c, p)
    s3 = s3.transpose(0, 2, 3, 5, 1, 4)
    s3 = jnp.stack([jnp.real(s3), -jnp.imag(s3)], axis=2)
    w3 = _block_diag_lanes(s3.reshape(na, ns, 2 * gs * p, 2, c), gs, p, c)
    zero = jnp.zeros_like(rcb)
    s1 = jnp.stack([jnp.stack([rcb, rclb], axis=1), jnp.stack([zero, rcb], axis=1)], axis=1)
    s1 = s1.reshape(na, 2, 2, ns, gs, c, c)
    s1 = s1.transpose(0, 3, 1, 4, 6, 2, 5)
    w1 = _block_diag_lanes(s1.reshape(na, ns, 2 * gs * c, 2, c), gs, c, c)
    lam2_t = jnp.stack([jnp.real(lam2).reshape(na, ns, gs * p),
                        jnp.imag(lam2).reshape(na, ns, gs * p)], axis=2)
    return w2.astype(BF16), w3.astype(BF16), w1.astype(BF16), lam2_t


def _s5_layer(h, nbatch, length, d, layer, batch_major_in, a_norm, w2, lam2, w3, w1, d_skip,
              wglu, b_glu):
    n_slabs, slab_ch = w2.shape[1], w2.shape[2] // 2
    nstate = w2.shape[3] // 2
    npairs = min(32, length // 2)
    half = npairs * nbatch
    tb = 2 * half
    if batch_major_in:
        n_in = d // V7X_LANES
        h_specs = [pl.BlockSpec((nbatch, 2 * npairs, V7X_LANES), functools.partial(
            lambda i, s: (0, i, s), s=s)) for s in range(n_in)]
        h_args = [h] * n_in
        extra = [pltpu.VMEM((n_in, tb, V7X_LANES), F32)]
    else:
        h_specs = [pl.BlockSpec((npairs, 2, nbatch, d), lambda i: (i, 0, 0, 0))]
        h_args = [h]
        extra = []
    est = (6 * tb * d * 4 + 2 * half * 2 * nstate * 4 + tb * d * 8 + 3 * tb * d * 2
           + n_slabs * slab_ch * 2 * (4 * nstate + 2 * slab_ch) * 2 + 2 * d * d * 2
           + 2 * tb * 2 * d * 4)
    kern = functools.partial(_s5_kernel, n_slabs=n_slabs, slab_ch=slab_ch, nstate=nstate,
                             nbatch=nbatch, npairs=npairs, batch_major_in=batch_major_in)
    return pl.pallas_call(
        kern,
        grid=(length // (2 * npairs),),
        in_specs=h_specs + [
            _layer_resident(a_norm, layer),
            _layer_resident(w2, layer),
            _layer_resident(lam2, layer),
            _layer_resident(w3, layer),
            _layer_resident(w1, layer),
            _layer_resident(d_skip, layer),
            _layer_resident(wglu, layer),
            _layer_resident(b_glu, layer),
        ],
        out_specs=pl.BlockSpec((npairs, 2, nbatch, d), lambda i: (i, 0, 0, 0)),
        out_shape=jax.ShapeDtypeStruct((length // 2, 2, nbatch, d), F32),
        scratch_shapes=[
            pltpu.VMEM((n_slabs, nbatch, 2 * nstate), F32),
            pltpu.VMEM((2, half, 2 * nstate), F32),
            pltpu.VMEM((tb, d), F32),
            pltpu.VMEM((tb, d), F32),
            pltpu.VMEM((n_slabs, half, 2 * slab_ch), BF16),
            pltpu.VMEM((tb, d), BF16),
        ] + extra,
        compiler_params=_params(("arbitrary",), est),
        name="s5_mixer",
    )(*h_args, a_norm, w2, lam2, w3, w1, d_skip, wglu, b_glu)


def _ffn_kernel(*refs, has_attn, shift, halo, tiles_per_seq, n_fc, fc, tm, n_slots,
                batch_major_out):
    if has_attn:
        (h_ref, o_ref, wo_ref, fnorm_ref, wup_ref, cw_ref, cb_ref, wdown_ref,
         out_ref, carry_ref, xnb_ref, act_ref, *hu) = refs
    else:
        (h_ref, fnorm_ref, wup_ref, cw_ref, cb_ref, wdown_ref,
         out_ref, carry_ref, xnb_ref, act_ref, *hu) = refs
    aligned = shift % V7X_SUBLANES == 0

    @pl.when(pl.program_id(0) % tiles_per_seq == 0)
    def _():
        carry_ref[...] = jnp.zeros_like(carry_ref)

    if batch_major_out:
        *hu, h1_ref, res_ref = hu
    else:
        h1_ref = out_ref
    h1 = h_ref[...]
    if has_attn:
        h1 = h1 + jnp.dot(o_ref[...], wo_ref[...], preferred_element_type=F32)
    h1_ref[...] = h1
    xnb_ref[...] = _rmsnorm(h1, fnorm_ref[...]).astype(BF16)

    def conv_branch(idx, slot):
        cols = slice(idx * fc, (idx + 1) * fc)
        u = jnp.dot(xnb_ref[...], wup_ref[:, cols], preferred_element_type=F32)
        w = cw_ref[:, cols]
        if aligned:
            prev = carry_ref[idx]
            carry_ref[idx] = u[tm - halo:, :]
            u1 = jnp.concatenate([prev[halo - shift:, :], u[:tm - shift, :]], axis=0)
            u2 = jnp.concatenate([prev[halo - 2 * shift:, :], u[:tm - 2 * shift, :]], axis=0)
        else:
            hu_ref = hu[0]
            hu_ref[slot, pl.ds(0, halo), :] = carry_ref[idx]
            hu_ref[slot, pl.ds(halo, tm), :] = u
            carry_ref[idx] = u[tm - halo:, :]
            u1 = hu_ref[slot, pl.ds(halo - shift, tm), :]
            u2 = hu_ref[slot, pl.ds(halo - 2 * shift, tm), :]
        return w[2:3, :] * u + w[1:2, :] * u1 + w[0:1, :] * u2 + cb_ref[:, cols]

    for c in range(n_fc):
        gate = conv_branch(c, (2 * c) % n_slots)
        val = conv_branch(n_fc + c, (2 * c + 1) % n_slots)
        act_ref[:, c * fc:(c + 1) * fc] = (jax.nn.silu(gate) * val).astype(BF16)

    res = h1_ref[...] + jnp.dot(act_ref[...], wdown_ref[...], preferred_element_type=F32)
    if batch_major_out:
        nbatch, tpb, d = out_ref.shape
        for s in range(d // V7X_LANES):
            res_ref[s] = res[:, s * V7X_LANES:(s + 1) * V7X_LANES]
        for b in range(nbatch):
            for s in range(d // V7X_LANES):
                out_ref[b, :, s * V7X_LANES:(s + 1) * V7X_LANES] = (
                    res_ref[s, pl.ds(b, tpb, stride=nbatch), :])
    else:
        out_ref[...] = res


def _ffn_layer(h, n_rows, d, shift, tiles_per_seq, tm, layer, ffn_norm, wup, conv_w,
               conv_b, wdown, attn=None, batch_major_out=False):
    f = wdown.shape[1]
    fc = V7X_MXU_DIM
    n_fc = f // fc
    halo = -(-2 * shift // V7X_SUBLANES) * V7X_SUBLANES
    aligned = shift % V7X_SUBLANES == 0
    n_slots = 4
    row_map = lambda i: (i, 0)
    in_specs = [pl.BlockSpec((tm, d), row_map)]
    args = [h]
    if attn is not None:
        o, wo, j = attn
        in_specs += [pl.BlockSpec((tm, d), row_map), _layer_resident(wo, j)]
        args += [o, wo]
    in_specs += [_layer_resident(ffn_norm, layer), _layer_resident(wup, layer),
                 _layer_resident(conv_w, layer), _layer_resident(conv_b, layer),
                 _layer_resident(wdown, layer)]
    args += [ffn_norm, wup, conv_w, conv_b, wdown]
    scratch = [
        pltpu.VMEM((2 * n_fc, halo, fc), F32),
        pltpu.VMEM((tm, d), BF16),
        pltpu.VMEM((tm, f), BF16),
    ]
    if not aligned:
        scratch.append(pltpu.VMEM((n_slots, tm + halo, fc), F32))
    if batch_major_out:
        scratch += [pltpu.VMEM((tm, d), F32), pltpu.VMEM((d // V7X_LANES, tm, V7X_LANES), F32)]
        out_spec = pl.BlockSpec((shift, tm // shift, d), lambda i: (0, i, 0))
        out_shape = jax.ShapeDtypeStruct((shift, n_rows // shift, d), F32)
    else:
        out_spec = pl.BlockSpec((tm, d), row_map)
        out_shape = jax.ShapeDtypeStruct((n_rows, d), F32)
    est = (d * 2 * f * 2 + f * d * 2 + d * d * 2 + 8 * tm * d * 4 + tm * d * 2 + tm * f * 2
           + n_slots * (tm + halo) * fc * 4 + 2 * n_fc * halo * fc * 4 + 8 * tm * fc * 4)
    kern = functools.partial(_ffn_kernel, has_attn=attn is not None, shift=shift, halo=halo,
                             tiles_per_seq=tiles_per_seq, n_fc=n_fc, fc=fc, tm=tm,
                             n_slots=n_slots, batch_major_out=batch_major_out)
    return pl.pallas_call(
        kern,
        grid=(n_rows // tm,),
        in_specs=in_specs,
        out_specs=out_spec,
        out_shape=out_shape,
        scratch_shapes=scratch,
        compiler_params=_params(("arbitrary",), est),
        name="conv_ffn_attn" if attn is not None else "conv_ffn",
    )(*args)


def _rope(x, cc, ss, dh):
    return x * cc + pltpu.roll(x, dh // 2, axis=1) * ss


def _kv_kernel(h_ref, nrm_ref, wkv_ref, knorm_ref, cc_ref, ss_ref,
               kaug_ref, v_ref, kmean_ref, *, n_heads, dh):
    xn = _rmsnorm(h_ref[...], nrm_ref[...]).astype(BF16)
    kv = jnp.dot(xn, wkv_ref[...], preferred_element_type=F32)
    rows = kv.shape[0]
    blk = pl.program_id(1)
    lane = lax.broadcasted_iota(jnp.int32, (rows, dh), 1)
    onehot = jnp.where(lane == blk, 1.0, 0.0).astype(BF16)
    cc = cc_ref[...]
    ss = ss_ref[...]
    means = []
    for hd in range(n_heads):
        kh = _rmsnorm(kv[:, hd * dh:(hd + 1) * dh], knorm_ref[...])
        kr = _rope(kh, cc, ss, dh)
        kaug_ref[0, hd, :, pl.ds(0, dh)] = kr.astype(BF16)
        kaug_ref[0, hd, :, pl.ds(dh, dh)] = onehot
        v_ref[0, hd] = kv[:, (n_heads + hd) * dh:(n_heads + hd + 1) * dh].astype(BF16)
        means.append(jnp.mean(kr, axis=0, keepdims=True))
    kmean_ref[0, 0] = jnp.concatenate(means, axis=0)


def _kv_proj(h, nbatch, length, d, kv_norm, w_kv, k_norm, cc, ss):
    dh = k_norm.shape[0]
    n_heads = w_kv.shape[1] // (2 * dh)
    nblk = length // MOBA_BLOCK
    t = MOBA_BLOCK
    est = w_kv.size * 2 + 2 * t * d * 4 + 3 * t * 2 * n_heads * dh * 4 + 6 * n_heads * t * dh * 2
    return pl.pallas_call(
        functools.partial(_kv_kernel, n_heads=n_heads, dh=dh),
        grid=(nbatch, nblk),
        in_specs=[
            pl.BlockSpec((t, d), lambda b, i: (b * nblk + i, 0)),
            _resident((1, d)),
            _resident(w_kv.shape),
            _resident((1, dh)),
            pl.BlockSpec((t, dh), lambda b, i: (i, 0)),
            pl.BlockSpec((t, dh), lambda b, i: (i, 0)),
        ],
        out_specs=[
            pl.BlockSpec((1, n_heads, t, 2 * dh), lambda b, i: (b, 0, i, 0)),
            pl.BlockSpec((1, n_heads, t, dh), lambda b, i: (b, 0, i, 0)),
            pl.BlockSpec((1, 1, n_heads, dh), lambda b, i: (b, i, 0, 0)),
        ],
        out_shape=[
            jax.ShapeDtypeStruct((nbatch, n_heads, length, 2 * dh), BF16),
            jax.ShapeDtypeStruct((nbatch, n_heads, length, dh), BF16),
            jax.ShapeDtypeStruct((nbatch, nblk, n_heads, dh), F32),
        ],
        compiler_params=_params(("parallel", "parallel"), est),
        name="kv_proj",
    )(h, kv_norm.reshape(1, d), w_kv.astype(BF16), k_norm.reshape(1, dh), cc, ss)


def _q_kernel(h_ref, nrm_ref, wq_ref, qnorm_ref, cc_ref, ss_ref, kmean_ref, qaug_ref,
              *, n_heads, dh, nblk, topk, scale):
    xn = _rmsnorm(h_ref[...], nrm_ref[...]).astype(BF16)
    q = jnp.dot(xn, wq_ref[...], preferred_element_type=F32)
    rows = q.shape[0]
    cur = pl.program_id(1)
    cc = cc_ref[...]
    ss = ss_ref[...]
    blk_id = lax.broadcasted_iota(jnp.int32, (nblk, rows), 0)
    for hd in range(n_heads):
        qh = _rmsnorm(q[:, hd * dh:(hd + 1) * dh], qnorm_ref[...])
        qr = _rope(qh, cc, ss, dh)
        gate_t = lax.dot_general(kmean_ref[0, hd], qr, (((1,), (1,)), ((), ())),
                                 precision=lax.Precision.HIGHEST,
                                 preferred_element_type=F32)
        gate_t = jnp.where(blk_id < cur, gate_t, -jnp.inf)
        sel_rows = []
        for j in range(nblk):
            gj = gate_t[j:j + 1, :]
            beats = jnp.where((gate_t > gj) | ((gate_t == gj) & (blk_id < j)), 1.0, 0.0)
            rank = jnp.sum(beats, axis=0, keepdims=True)
            sel_rows.append(jnp.where(((rank < topk) & (j < cur)) | (j == cur), 1.0, 0.0))
        sel_t = jnp.concatenate(
            sel_rows + [jnp.zeros((dh - nblk, rows), F32)], axis=0)
        sel = sel_t.T
        qaug_ref[0, hd, :, pl.ds(0, dh)] = (qr * scale).astype(BF16)
        qaug_ref[0, hd, :, pl.ds(dh, dh)] = jnp.where(sel > 0.5, 0.0, MASK_BIAS).astype(BF16)


def _q_proj(h, nbatch, length, d, b_norm, w_q, q_norm, cc, ss, kmean):
    dh = q_norm.shape[0]
    n_heads = w_q.shape[1] // dh
    nblk = length // MOBA_BLOCK
    t = MOBA_BLOCK
    est = w_q.size * 2 + 2 * t * d * 4 + 3 * t * n_heads * dh * 4 + 4 * n_heads * t * dh * 2
    kern = functools.partial(_q_kernel, n_heads=n_heads, dh=dh, nblk=nblk,
                             topk=min(MOBA_TOPK, nblk), scale=dh ** -0.5)
    return pl.pallas_call(
        kern,
        grid=(nbatch, nblk),
        in_specs=[
            pl.BlockSpec((t, d), lambda b, i: (b * nblk + i, 0)),
            _resident((1, d)),
            _resident(w_q.shape),
            _resident((1, dh)),
            pl.BlockSpec((t, dh), lambda b, i: (i, 0)),
            pl.BlockSpec((t, dh), lambda b, i: (i, 0)),
            pl.BlockSpec((1, n_heads, nblk, dh), lambda b, i: (b, 0, 0, 0)),
        ],
        out_specs=pl.BlockSpec((1, n_heads, t, 2 * dh), lambda b, i: (b, 0, i, 0)),
        out_shape=jax.ShapeDtypeStruct((nbatch, n_heads, length, 2 * dh), BF16),
        compiler_params=_params(("parallel", "parallel"), est),
        name="q_proj_gate",
    )(h, b_norm.reshape(1, d), w_q.astype(BF16), q_norm.reshape(1, dh), cc, ss, kmean)


def _attn_kernel(q_ref, k_ref, v_ref, o_ref, s_ref, *, blk, nblk):
    nt = (((1,), (1,)), ((), ()))
    qpos = lax.broadcasted_iota(jnp.int32, (blk, blk), 0)
    kpos = lax.broadcasted_iota(jnp.int32, (blk, blk), 1)
    causal = kpos <= qpos
    for c in range(nblk):
        slot = c % 2
        q = q_ref[0, 0, c * blk:(c + 1) * blk, :]
        m_acc = None
        for j in range(c + 1):
            s = lax.dot_general(q, k_ref[0, 0, j * blk:(j + 1) * blk, :], nt,
                                preferred_element_type=F32)
            if j == c:
                s = jnp.where(causal, s, -jnp.inf)
            s_ref[slot, :, j * blk:(j + 1) * blk] = s
            m_acc = s if m_acc is None else jnp.maximum(m_acc, s)
        m = jnp.max(m_acc, axis=1, keepdims=True)
        l_acc = None
        acc = None
        for j in range(c + 1):
            p = jnp.exp(s_ref[slot, :, j * blk:(j + 1) * blk] - m)
            pv = jnp.dot(p.astype(BF16), v_ref[0, 0, j * blk:(j + 1) * blk, :],
                         preferred_element_type=F32)
            l_acc = p if l_acc is None else l_acc + p
            acc = pv if acc is None else acc + pv
        l = jnp.sum(l_acc, axis=1, keepdims=True)
        o_ref[c * blk:(c + 1) * blk, :] = (acc / l).astype(o_ref.dtype)


def _attention(qaug, kaug, v):
    nbatch, n_heads, length, dh2 = qaug.shape
    dh = dh2 // 2
    blk = MOBA_BLOCK
    nblk = length // blk
    est = 4 * length * dh2 * 2 + 4 * length * dh * 2 + 2 * blk * length * 4
    return pl.pallas_call(
        functools.partial(_attn_kernel, blk=blk, nblk=nblk),
        grid=(nbatch, n_heads),
        in_specs=[
            pl.BlockSpec((1, 1, length, dh2), lambda b, h: (b, h, 0, 0)),
            pl.BlockSpec((1, 1, length, dh2), lambda b, h: (b, h, 0, 0)),
            pl.BlockSpec((1, 1, length, dh), lambda b, h: (b, h, 0, 0)),
        ],
        out_specs=pl.BlockSpec((length, dh), lambda b, h: (b, h)),
        out_shape=jax.ShapeDtypeStruct((nbatch * length, n_heads * dh), BF16),
        scratch_shapes=[pltpu.VMEM((2, blk, length), F32)],
        compiler_params=_params(("parallel", "parallel"), est),
        name="moba_attention",
    )(qaug, kaug, v)


def _rope_tables(length, dh):
    half = dh // 2
    inv_freq = ROPE_THETA ** (-jnp.arange(half, dtype=F32) * 2.0 / dh)
    ang = jnp.arange(length, dtype=F32)[:, None] * inv_freq[None, :]
    cos, sin = jnp.cos(ang), jnp.sin(ang)
    return jnp.concatenate([cos, cos], axis=-1), jnp.concatenate([-sin, sin], axis=-1)


def kernel(x, a_norm, s5_A_re, s5_A_im, s5_log_dt, s5_B_re, s5_B_im, s5_C_re, s5_C_im, s5_D,
           w_glu, b_glu, kv_norm, w_kv, k_norm, b_norm, w_q, q_norm, w_o, ffn_norm, w_up,
           conv_w, conv_b, w_down):
    nbatch, length, d = x.shape
    n_a = a_norm.shape[0]
    n_b = b_norm.shape[0]
    rows = nbatch * length
    assert nbatch == V7X_SUBLANES, "the S5 recurrence keeps the batch on the sublane axis"
    assert length % MOBA_BLOCK == 0 and n_a >= 1 and n_b >= 1
    dh = k_norm.shape[0]
    cc, ss = _rope_tables(length, dh)

    depth = n_a + n_b
    g = s5_A_re.shape[1]
    w2, w3, w1, lam2 = _s5_prepare(s5_A_re, s5_A_im, s5_log_dt, s5_B_re, s5_B_im, s5_C_re,
                                   s5_C_im, min(g, V7X_LANES // (d // g)))
    a_norm3, d_skip3 = a_norm.reshape(n_a, 1, d), s5_D.reshape(n_a, 1, d)
    wglu, b_glu3 = w_glu.astype(BF16), b_glu.reshape(n_a, 1, 2 * d)
    fnorm3 = ffn_norm.reshape(depth, 1, d)
    wup, wdown = w_up.astype(BF16), w_down.astype(BF16)
    conv_b3 = conv_b.reshape(depth, 1, conv_b.shape[-1])
    wo = w_o.astype(BF16)

    h = x
    tm_a = min(512, rows)
    for i in range(n_a):
        h = _s5_layer(h, nbatch, length, d, i, i == 0, a_norm3, w2, lam2, w3, w1, d_skip3,
                      wglu, b_glu3)
        h = _ffn_layer(h.reshape(rows, d), rows, d, nbatch, rows // tm_a, tm_a, i, fnorm3, wup,
                       conv_w, conv_b3, wdown, batch_major_out=(i == n_a - 1))
        if i < n_a - 1:
            h = h.reshape(length // 2, 2, nbatch, d)

    h = h.reshape(rows, d)
    kaug, v, kmean = _kv_proj(h, nbatch, length, d, kv_norm, w_kv, k_norm, cc, ss)
    kmean = kmean.transpose(0, 2, 1, 3)
    tm_b = min(512, length)
    for j in range(n_b):
        qaug = _q_proj(h, nbatch, length, d, b_norm[j], w_q[j], q_norm[j], cc, ss, kmean)
        o = _attention(qaug, kaug, v)
        h = _ffn_layer(h, rows, d, 1, length // tm_b, tm_b, n_a + j, fnorm3, wup, conv_w,
                       conv_b3, wdown, attn=(o, wo, j))
    return h.reshape(nbatch, length, d)
```

```python
import functools

import jax
import jax.numpy as jnp
from jax import lax
from jax.experimental import pallas as pl
from jax.experimental.pallas import tpu as pltpu

MOBA_BLOCK = 256
MOBA_TOPK = 3
CONV_WIDTH = 3
ROPE_THETA = 10000.0
EPS = 1e-6

V7X_SUBLANES = 8
V7X_LANES = 128
V7X_MXU_DIM = 256
V7X_VMEM_BYTES = 64 * 1024 * 1024

MASK_BIAS = -1e30

F32 = jnp.float32
BF16 = jnp.bfloat16


def _vmem_limit(est_bytes):
    return int(min(V7X_VMEM_BYTES - 6 * 1024 * 1024, max(32 * 1024 * 1024, 2 * est_bytes)))


def _params(semantics, est_bytes):
    return pltpu.CompilerParams(dimension_semantics=semantics,
                                vmem_limit_bytes=_vmem_limit(est_bytes))


def _resident(shape):
    nd = len(shape)
    return pl.BlockSpec(shape, lambda *_: (0,) * nd, pipeline_mode=pl.Buffered(1))


def _layer_resident(arr, layer):
    nd = arr.ndim - 1
    return pl.BlockSpec((None,) + arr.shape[1:], lambda *_: (layer,) + (0,) * nd,
                        pipeline_mode=pl.Buffered(1))


def _rmsnorm(x, g):
    return x * lax.rsqrt(jnp.mean(x * x, axis=-1, keepdims=True) + EPS) * g


def _s5_kernel(*refs, n_slabs, slab_ch, nstate, nbatch, npairs, batch_major_in):
    n_in = n_slabs * slab_ch // V7X_LANES if batch_major_in else 1
    h_refs = refs[:n_in]
    (anorm_ref, w2_ref, lam2_ref, w3_ref, w1_ref, dskip_ref, wglu_ref, bglu_ref,
     out_ref, state_ref, z_ref, x_ref, xn_ref, pair_ref, g_ref, *xt) = refs[n_in:]
    half = npairs * nbatch
    d = x_ref.shape[-1]

    @pl.when(pl.program_id(0) == 0)
    def _():
        state_ref[...] = jnp.zeros_like(state_ref)

    if batch_major_in:
        xt_ref = xt[0]
        for s in range(n_in):
            for b in range(nbatch):
                for par in range(2):
                    xt_ref[s, pl.ds(par * half + b, npairs, stride=nbatch), :] = (
                        h_refs[s][b, pl.ds(par, npairs, stride=2), :])
            x_ref[:, s * V7X_LANES:(s + 1) * V7X_LANES] = xt_ref[s]
    else:
        h_ref = h_refs[0]
        x_ref[pl.ds(0, half), :] = h_ref[:, 0].reshape(half, d)
        x_ref[pl.ds(half, half), :] = h_ref[:, 1].reshape(half, d)

    xn = _rmsnorm(x_ref[...], anorm_ref[...])
    xn_ref[...] = xn
    xnb = xn.astype(BF16)

    for s in range(n_slabs):
        cs = slice(s * slab_ch, (s + 1) * slab_ch)
        pair_ref[s] = jnp.concatenate([xnb[:half, cs], xnb[half:, cs]], axis=1)
        zb = z_ref.at[s % 2]
        zb[...] = jnp.dot(pair_ref[s], w2_ref[s], preferred_element_type=F32)
        lr = jnp.broadcast_to(lam2_ref[s, 0:1, :], (nbatch, nstate))
        li = jnp.broadcast_to(lam2_ref[s, 1:2, :], (nbatch, nstate))
        sr = state_ref[s, :, pl.ds(0, nstate)]
        si = state_ref[s, :, pl.ds(nstate, nstate)]
        for k in range(npairs):
            rows = pl.ds(k * nbatch, nbatch)
            zr = zb[rows, pl.ds(0, nstate)]
            zi = zb[rows, pl.ds(nstate, nstate)]
            zb[rows, pl.ds(0, nstate)] = sr
            zb[rows, pl.ds(nstate, nstate)] = si
            sr, si = lr * sr - li * si + zr, lr * si + li * sr + zi
        state_ref[s, :, pl.ds(0, nstate)] = sr
        state_ref[s, :, pl.ds(nstate, nstate)] = si

        y = (jnp.dot(zb[...].astype(BF16), w3_ref[s], preferred_element_type=F32)
             + jnp.dot(pair_ref[s], w1_ref[s], preferred_element_type=F32))
        dsk = dskip_ref[:, cs]
        g_ref[pl.ds(0, half), cs] = jax.nn.gelu(
            y[:, :slab_ch] + dsk * xn_ref[pl.ds(0, half), cs]).astype(BF16)
        g_ref[pl.ds(half, half), cs] = jax.nn.gelu(
            y[:, slab_ch:] + dsk * xn_ref[pl.ds(half, half), cs]).astype(BF16)

    z = jnp.dot(g_ref[...], wglu_ref[...], preferred_element_type=F32) + bglu_ref[...]
    res = x_ref[...] + z[:, :d] * jax.nn.sigmoid(z[:, d:])
    out_ref[:, 0] = res[:half].reshape(npairs, nbatch, d)
    out_ref[:, 1] = res[half:].reshape(npairs, nbatch, d)


def _block_diag_lanes(src, n_blocks, row_block, col_block):
    rows = src.shape[-3]
    tiled = jnp.tile(src, (1,) * (src.ndim - 1) + (n_blocks,))
    g = (lax.broadcasted_iota(jnp.int32, (rows, n_blocks * col_block), 0) // row_block) % n_blocks
    h = lax.broadcasted_iota(jnp.int32, (rows, n_blocks * col_block), 1) // col_block
    out = jnp.where((g == h)[:, None, :], tiled, 0.0)
    return out.reshape(src.shape[:-2] + (src.shape[-2] * n_blocks * col_block,))


def _s5_prepare(a_re, a_im, log_dt, b_re, b_im, c_re, c_im, gs):
    na, g, p = a_re.shape
    c = b_re.shape[-1]
    ns = g // gs
    hi = lax.Precision.HIGHEST
    lam = lax.complex(a_re, a_im)
    dt = jnp.exp(log_dt)[..., None]
    lam_bar = jnp.exp(lam * dt)
    b_bar = ((lam_bar - 1.0) / lam)[..., None] * lax.complex(b_re, b_im)
    c_mat = lax.complex(c_re, c_im)
    lb = lam_bar[..., None] * b_bar
    cl1 = c_mat * lam_bar[:, :, None, :]
    cl2 = cl1 * lam_bar[:, :, None, :]
    lam2 = lam_bar * lam_bar

    def re_prod(cm, bm):
        return (jnp.einsum('lgdp,lgpc->lgdc', jnp.real(cm), jnp.real(bm), precision=hi)
                - jnp.einsum('lgdp,lgpc->lgdc', jnp.imag(cm), jnp.imag(bm), precision=hi))

    rcb = re_prod(c_mat, b_bar)
    rclb = re_prod(cl1, b_bar)

    s2 = jnp.stack([lb, b_bar], axis=1).reshape(na, 2, ns, gs, p, c)
    s2 = s2.transpose(0, 2, 1, 3, 5, 4)
    s2 = jnp.stack([jnp.real(s2), jnp.imag(s2)], axis=-2)
    w2 = _block_diag_lanes(s2.reshape(na, ns, 2 * gs * c, 2, p), gs, c, p)
    s3 = jnp.stack([cl1, cl2], axis=1).reshape(na, 2, ns, gs, c, p)
    s3 = s3.transpose(0, 2, 3, 5, 1, 4)
    s3 = jnp.stack([jnp.real(s3), -jnp.imag(s3)], axis=2)
    w3 = _block_diag_lanes(s3.reshape(na, ns, 2 * gs * p, 2, c), gs, p, c)
    zero = jnp.zeros_like(rcb)
    s1 = jnp.stack([jnp.stack([rcb, rclb], axis=1), jnp.stack([zero, rcb], axis=1)], axis=1)
    s1 = s1.reshape(na, 2, 2, ns, gs, c, c)
    s1 = s1.transpose(0, 3, 1, 4, 6, 2, 5)
    w1 = _block_diag_lanes(s1.reshape(na, ns, 2 * gs * c, 2, c), gs, c, c)
    lam2_t = jnp.stack([jnp.real(lam2).reshape(na, ns, gs * p),
                        jnp.imag(lam2).reshape(na, ns, gs * p)], axis=2)
    return w2.astype(BF16), w3.astype(BF16), w1.astype(BF16), lam2_t


def _s5_layer(h, nbatch, length, d, layer, batch_major_in, a_norm, w2, lam2, w3, w1, d_skip,
              wglu, b_glu):
    n_slabs, slab_ch = w2.shape[1], w2.shape[2] // 2
    nstate = w2.shape[3] // 2
    npairs = min(32, length // 2)
    half = npairs * nbatch
    tb = 2 * half
    if batch_major_in:
        n_in = d // V7X_LANES
        h_specs = [pl.BlockSpec((nbatch, 2 * npairs, V7X_LANES), functools.partial(
            lambda i, s: (0, i, s), s=s)) for s in range(n_in)]
        h_args = [h] * n_in
        extra = [pltpu.VMEM((n_in, tb, V7X_LANES), F32)]
    else:
        h_specs = [pl.BlockSpec((npairs, 2, nbatch, d), lambda i: (i, 0, 0, 0))]
        h_args = [h]
        extra = []
    est = (6 * tb * d * 4 + 2 * half * 2 * nstate * 4 + tb * d * 8 + 3 * tb * d * 2
           + n_slabs * slab_ch * 2 * (4 * nstate + 2 * slab_ch) * 2 + 2 * d * d * 2
           + 2 * tb * 2 * d * 4)
    kern = functools.partial(_s5_kernel, n_slabs=n_slabs, slab_ch=slab_ch, nstate=nstate,
                             nbatch=nbatch, npairs=npairs, batch_major_in=batch_major_in)
    return pl.pallas_call(
        kern,
        grid=(length // (2 * npairs),),
        in_specs=h_specs + [
            _layer_resident(a_norm, layer),
            _layer_resident(w2, layer),
            _layer_resident(lam2, layer),
            _layer_resident(w3, layer),
            _layer_resident(w1, layer),
            _layer_resident(d_skip, layer),
            _layer_resident(wglu, layer),
            _layer_resident(b_glu, layer),
        ],
        out_specs=pl.BlockSpec((npairs, 2, nbatch, d), lambda i: (i, 0, 0, 0)),
        out_shape=jax.ShapeDtypeStruct((length // 2, 2, nbatch, d), F32),
        scratch_shapes=[
            pltpu.VMEM((n_slabs, nbatch, 2 * nstate), F32),
            pltpu.VMEM((2, half, 2 * nstate), F32),
            pltpu.VMEM((tb, d), F32),
            pltpu.VMEM((tb, d), F32),
            pltpu.VMEM((n_slabs, half, 2 * slab_ch), BF16),
            pltpu.VMEM((tb, d), BF16),
        ] + extra,
        compiler_params=_params(("arbitrary",), est),
        name="s5_mixer",
    )(*h_args, a_norm, w2, lam2, w3, w1, d_skip, wglu, b_glu)


def _ffn_kernel(*refs, has_attn, shift, halo, tiles_per_seq, n_fc, fc, tm, n_slots,
                batch_major_out):
    if has_attn:
        (h_ref, o_ref, wo_ref, fnorm_ref, wup_ref, cw_ref, cb_ref, wdown_ref,
         out_ref, carry_ref, xnb_ref, act_ref, *hu) = refs
    else:
        (h_ref, fnorm_ref, wup_ref, cw_ref, cb_ref, wdown_ref,
         out_ref, carry_ref, xnb_ref, act_ref, *hu) = refs
    aligned = shift % V7X_SUBLANES == 0

    @pl.when(pl.program_id(0) % tiles_per_seq == 0)
    def _():
        carry_ref[...] = jnp.zeros_like(carry_ref)

    if batch_major_out:
        *hu, h1_ref, res_ref = hu
    else:
        h1_ref = out_ref
    h1 = h_ref[...]
    if has_attn:
        h1 = h1 + jnp.dot(o_ref[...], wo_ref[...], preferred_element_type=F32)
    h1_ref[...] = h1
    xnb_ref[...] = _rmsnorm(h1, fnorm_ref[...]).astype(BF16)

    def conv_branch(idx, slot):
        cols = slice(idx * fc, (idx + 1) * fc)
        u = jnp.dot(xnb_ref[...], wup_ref[:, cols], preferred_element_type=F32)
        w = cw_ref[:, cols]
        if aligned:
            prev = carry_ref[idx]
            carry_ref[idx] = u[tm - halo:, :]
            u1 = jnp.concatenate([prev[halo - shift:, :], u[:tm - shift, :]], axis=0)
            u2 = jnp.concatenate([prev[halo - 2 * shift:, :], u[:tm - 2 * shift, :]], axis=0)
        else:
            hu_ref = hu[0]
            hu_ref[slot, pl.ds(0, halo), :] = carry_ref[idx]
            hu_ref[slot, pl.ds(halo, tm), :] = u
            carry_ref[idx] = u[tm - halo:, :]
            u1 = hu_ref[slot, pl.ds(halo - shift, tm), :]
            u2 = hu_ref[slot, pl.ds(halo - 2 * shift, tm), :]
        return w[2:3, :] * u + w[1:2, :] * u1 + w[0:1, :] * u2 + cb_ref[:, cols]

    for c in range(n_fc):
        gate = conv_branch(c, (2 * c) % n_slots)
        val = conv_branch(n_fc + c, (2 * c + 1) % n_slots)
        act_ref[:, c * fc:(c + 1) * fc] = (jax.nn.silu(gate) * val).astype(BF16)

    res = h1_ref[...] + jnp.dot(act_ref[...], wdown_ref[...], preferred_element_type=F32)
    if batch_major_out:
        nbatch, tpb, d = out_ref.shape
        for s in range(d // V7X_LANES):
            res_ref[s] = res[:, s * V7X_LANES:(s + 1) * V7X_LANES]
        for b in range(nbatch):
            for s in range(d // V7X_LANES):
                out_ref[b, :, s * V7X_LANES:(s + 1) * V7X_LANES] = (
                    res_ref[s, pl.ds(b, tpb, stride=nbatch), :])
    else:
        out_ref[...] = res


def _ffn_layer(h, n_rows, d, shift, tiles_per_seq, tm, layer, ffn_norm, wup, conv_w,
               conv_b, wdown, attn=None, batch_major_out=False):
    f = wdown.shape[1]
    fc = V7X_MXU_DIM
    n_fc = f // fc
    halo = -(-2 * shift // V7X_SUBLANES) * V7X_SUBLANES
    aligned = shift % V7X_SUBLANES == 0
    n_slots = 4
    row_map = lambda i: (i, 0)
    in_specs = [pl.BlockSpec((tm, d), row_map)]
    args = [h]
    if attn is not None:
        o, wo, j = attn
        in_specs += [pl.BlockSpec((tm, d), row_map), _layer_resident(wo, j)]
        args += [o, wo]
    in_specs += [_layer_resident(ffn_norm, layer), _layer_resident(wup, layer),
                 _layer_resident(conv_w, layer), _layer_resident(conv_b, layer),
                 _layer_resident(wdown, layer)]
    args += [ffn_norm, wup, conv_w, conv_b, wdown]
    scratch = [
        pltpu.VMEM((2 * n_fc, halo, fc), F32),
        pltpu.VMEM((tm, d), BF16),
        pltpu.VMEM((tm, f), BF16),
    ]
    if not aligned:
        scratch.append(pltpu.VMEM((n_slots, tm + halo, fc), F32))
    if batch_major_out:
        scratch += [pltpu.VMEM((tm, d), F32), pltpu.VMEM((d // V7X_LANES, tm, V7X_LANES), F32)]
        out_spec = pl.BlockSpec((shift, tm // shift, d), lambda i: (0, i, 0))
        out_shape = jax.ShapeDtypeStruct((shift, n_rows // shift, d), F32)
    else:
        out_spec = pl.BlockSpec((tm, d), row_map)
        out_shape = jax.ShapeDtypeStruct((n_rows, d), F32)
    est = (d * 2 * f * 2 + f * d * 2 + d * d * 2 + 8 * tm * d * 4 + tm * d * 2 + tm * f * 2
           + n_slots * (tm + halo) * fc * 4 + 2 * n_fc * halo * fc * 4 + 8 * tm * fc * 4)
    kern = functools.partial(_ffn_kernel, has_attn=attn is not None, shift=shift, halo=halo,
                             tiles_per_seq=tiles_per_seq, n_fc=n_fc, fc=fc, tm=tm,
                             n_slots=n_slots, batch_major_out=batch_major_out)
    return pl.pallas_call(
        kern,
        grid=(n_rows // tm,),
        in_specs=in_specs,
        out_specs=out_spec,
        out_shape=out_shape,
        scratch_shapes=scratch,
        compiler_params=_params(("arbitrary",), est),
        name="conv_ffn_attn" if attn is not None else "conv_ffn",
    )(*args)


def _rope(x, cc, ss, dh):
    return x * cc + pltpu.roll(x, dh // 2, axis=1) * ss


def _kv_kernel(h_ref, nrm_ref, wk_ref, wvt_ref, knorm_ref, cc_ref, ss_ref,
               kaug_ref, vt_ref, kmean_ref, *, n_heads, dh):
    xn = _rmsnorm(h_ref[...], nrm_ref[...]).astype(BF16)
    kv = jnp.dot(xn, wk_ref[...], preferred_element_type=F32)
    vt = lax.dot_general(wvt_ref[...], xn, (((1,), (1,)), ((), ())),
                         preferred_element_type=F32)
    rows = kv.shape[0]
    blk = pl.program_id(1)
    lane = lax.broadcasted_iota(jnp.int32, (rows, dh), 1)
    onehot = jnp.where(lane == blk, 1.0, 0.0).astype(BF16)
    cc = cc_ref[...]
    ss = ss_ref[...]
    means = []
    for hd in range(n_heads):
        kh = _rmsnorm(kv[:, hd * dh:(hd + 1) * dh], knorm_ref[...])
        kr = _rope(kh, cc, ss, dh)
        kaug_ref[0, hd, :, pl.ds(0, dh)] = kr.astype(BF16)
        kaug_ref[0, hd, :, pl.ds(dh, dh)] = onehot
        vt_ref[0, hd] = vt[hd * dh:(hd + 1) * dh, :].astype(BF16)
        means.append(jnp.mean(kr, axis=0, keepdims=True))
    kmean_ref[0, 0] = jnp.concatenate(means, axis=0)


def _kv_proj(h, nbatch, length, d, kv_norm, w_kv, k_norm, cc, ss):
    dh = k_norm.shape[0]
    n_heads = w_kv.shape[1] // (2 * dh)
    nblk = length // MOBA_BLOCK
    t = MOBA_BLOCK
    est = w_kv.size * 2 + 2 * t * d * 4 + 3 * t * 2 * n_heads * dh * 4 + 6 * n_heads * t * dh * 2
    return pl.pallas_call(
        functools.partial(_kv_kernel, n_heads=n_heads, dh=dh),
        grid=(nbatch, nblk),
        in_specs=[
            pl.BlockSpec((t, d), lambda b, i: (b * nblk + i, 0)),
            _resident((1, d)),
            _resident((d, n_heads * dh)),
            _resident((n_heads * dh, d)),
            _resident((1, dh)),
            pl.BlockSpec((t, dh), lambda b, i: (i, 0)),
            pl.BlockSpec((t, dh), lambda b, i: (i, 0)),
        ],
        out_specs=[
            pl.BlockSpec((1, n_heads, t, 2 * dh), lambda b, i: (b, 0, i, 0)),
            pl.BlockSpec((1, n_heads, dh, t), lambda b, i: (b, 0, 0, i)),
            pl.BlockSpec((1, 1, n_heads, dh), lambda b, i: (b, i, 0, 0)),
        ],
        out_shape=[
            jax.ShapeDtypeStruct((nbatch, n_heads, length, 2 * dh), BF16),
            jax.ShapeDtypeStruct((nbatch, n_heads, dh, length), BF16),
            jax.ShapeDtypeStruct((nbatch, nblk, n_heads, dh), F32),
        ],
        compiler_params=_params(("parallel", "parallel"), est),
        name="kv_proj",
    )(h, kv_norm.reshape(1, d), w_kv[:, :n_heads * dh].astype(BF16),
      w_kv[:, n_heads * dh:].T.astype(BF16), k_norm.reshape(1, dh), cc, ss)


def _q_kernel(h_ref, nrm_ref, wq_ref, qnorm_ref, cc_ref, ss_ref, kmean_ref, qaug_ref,
              *, n_heads, dh, nblk, topk, scale):
    xn = _rmsnorm(h_ref[...], nrm_ref[...]).astype(BF16)
    q = jnp.dot(xn, wq_ref[...], preferred_element_type=F32)
    rows = q.shape[0]
    cur = pl.program_id(1)
    cc = cc_ref[...]
    ss = ss_ref[...]
    blk_id = lax.broadcasted_iota(jnp.int32, (nblk, rows), 0)
    for hd in range(n_heads):
        qh = _rmsnorm(q[:, hd * dh:(hd + 1) * dh], qnorm_ref[...])
        qr = _rope(qh, cc, ss, dh)
        gate_t = lax.dot_general(kmean_ref[0, hd], qr, (((1,), (1,)), ((), ())),
                                 precision=lax.Precision.HIGHEST,
                                 preferred_element_type=F32)
        gate_t = jnp.where(blk_id < cur, gate_t, -jnp.inf)
        sel_rows = []
        for j in range(nblk):
            gj = gate_t[j:j + 1, :]
            beats = jnp.where((gate_t > gj) | ((gate_t == gj) & (blk_id < j)), 1.0, 0.0)
            rank = jnp.sum(beats, axis=0, keepdims=True)
            sel_rows.append(jnp.where(((rank < topk) & (j < cur)) | (j == cur), 1.0, 0.0))
        sel_t = jnp.concatenate(
            sel_rows + [jnp.zeros((dh - nblk, rows), F32)], axis=0)
        sel = sel_t.T
        qaug_ref[0, hd, :, pl.ds(0, dh)] = (qr * scale).astype(BF16)
        qaug_ref[0, hd, :, pl.ds(dh, dh)] = jnp.where(sel > 0.5, 0.0, MASK_BIAS).astype(BF16)


def _q_proj(h, nbatch, length, d, b_norm, w_q, q_norm, cc, ss, kmean):
    dh = q_norm.shape[0]
    n_heads = w_q.shape[1] // dh
    nblk = length // MOBA_BLOCK
    t = MOBA_BLOCK
    est = w_q.size * 2 + 2 * t * d * 4 + 3 * t * n_heads * dh * 4 + 4 * n_heads * t * dh * 2
    kern = functools.partial(_q_kernel, n_heads=n_heads, dh=dh, nblk=nblk,
                             topk=min(MOBA_TOPK, nblk), scale=dh ** -0.5)
    return pl.pallas_call(
        kern,
        grid=(nbatch, nblk),
        in_specs=[
            pl.BlockSpec((t, d), lambda b, i: (b * nblk + i, 0)),
            _resident((1, d)),
            _resident(w_q.shape),
            _resident((1, dh)),
            pl.BlockSpec((t, dh), lambda b, i: (i, 0)),
            pl.BlockSpec((t, dh), lambda b, i: (i, 0)),
            pl.BlockSpec((1, n_heads, nblk, dh), lambda b, i: (b, 0, 0, 0)),
        ],
        out_specs=pl.BlockSpec((1, n_heads, t, 2 * dh), lambda b, i: (b, 0, i, 0)),
        out_shape=jax.ShapeDtypeStruct((nbatch, n_heads, length, 2 * dh), BF16),
        compiler_params=_params(("parallel", "parallel"), est),
        name="q_proj_gate",
    )(h, b_norm.reshape(1, d), w_q.astype(BF16), q_norm.reshape(1, dh), cc, ss, kmean)


def _attn_kernel(q_ref, k_ref, vt_ref, o_ref, s_ref, *, blk, nblk):
    nt = (((1,), (1,)), ((), ()))
    kpos = lax.broadcasted_iota(jnp.int32, (blk, blk), 0)
    qpos = lax.broadcasted_iota(jnp.int32, (blk, blk), 1)
    causal = kpos <= qpos

    def score_step(c, j, m_acc):
        st = lax.dot_general(k_ref[0, 0, j * blk:(j + 1) * blk, :],
                             q_ref[0, 0, c * blk:(c + 1) * blk, :], nt,
                             preferred_element_type=F32)
        if j == c:
            st = jnp.where(causal, st, -jnp.inf)
        s_ref[c % 2, j * blk:(j + 1) * blk, :] = st
        part = jnp.max(st.reshape(blk // V7X_SUBLANES, V7X_SUBLANES, blk), axis=0)
        return part if m_acc is None else jnp.maximum(m_acc, part)

    def value_step(c, j, m, l_acc, acc):
        pt = jnp.exp(s_ref[c % 2, j * blk:(j + 1) * blk, :] - m)
        pv = jnp.dot(vt_ref[0, 0, :, j * blk:(j + 1) * blk], pt.astype(BF16),
                     preferred_element_type=F32)
        part = jnp.sum(pt.reshape(blk // V7X_SUBLANES, V7X_SUBLANES, blk), axis=0)
        return (part if l_acc is None else l_acc + part), (pv if acc is None else acc + pv)

    m_acc = score_step(0, 0, None)
    m = jnp.max(m_acc, axis=0, keepdims=True)
    for c in range(nblk):
        l_acc = acc = m_next = None
        for j in range(c + 2):
            if c + 1 < nblk:
                m_next = score_step(c + 1, j, m_next)
            if j <= c:
                l_acc, acc = value_step(c, j, m, l_acc, acc)
        l = jnp.sum(l_acc, axis=0, keepdims=True)
        o_ref[c * blk:(c + 1) * blk, :] = (acc / l).T.astype(o_ref.dtype)
        if c + 1 < nblk:
            m = jnp.max(m_next, axis=0, keepdims=True)


def _attention(qaug, kaug, vt):
    nbatch, n_heads, length, dh2 = qaug.shape
    dh = dh2 // 2
    blk = MOBA_BLOCK
    nblk = length // blk
    est = 4 * length * dh2 * 2 + 4 * length * dh * 2 + 2 * blk * length * 4
    return pl.pallas_call(
        functools.partial(_attn_kernel, blk=blk, nblk=nblk),
        grid=(nbatch, n_heads),
        in_specs=[
            pl.BlockSpec((1, 1, length, dh2), lambda b, h: (b, h, 0, 0)),
            pl.BlockSpec((1, 1, length, dh2), lambda b, h: (b, h, 0, 0)),
            pl.BlockSpec((1, 1, dh, length), lambda b, h: (b, h, 0, 0)),
        ],
        out_specs=pl.BlockSpec((length, dh), lambda b, h: (b, h)),
        out_shape=jax.ShapeDtypeStruct((nbatch * length, n_heads * dh), BF16),
        scratch_shapes=[pltpu.VMEM((2, length, blk), F32)],
        compiler_params=_params(("parallel", "parallel"), est),
        name="moba_attention",
    )(qaug, kaug, vt)


def _rope_tables(length, dh):
    half = dh // 2
    inv_freq = ROPE_THETA ** (-jnp.arange(half, dtype=F32) * 2.0 / dh)
    ang = jnp.arange(length, dtype=F32)[:, None] * inv_freq[None, :]
    cos, sin = jnp.cos(ang), jnp.sin(ang)
    return jnp.concatenate([cos, cos], axis=-1), jnp.concatenate([-sin, sin], axis=-1)


def kernel(x, a_norm, s5_A_re, s5_A_im, s5_log_dt, s5_B_re, s5_B_im, s5_C_re, s5_C_im, s5_D,
           w_glu, b_glu, kv_norm, w_kv, k_norm, b_norm, w_q, q_norm, w_o, ffn_norm, w_up,
           conv_w, conv_b, w_down):
    nbatch, length, d = x.shape
    n_a = a_norm.shape[0]
    n_b = b_norm.shape[0]
    rows = nbatch * length
    assert nbatch == V7X_SUBLANES, "the S5 recurrence keeps the batch on the sublane axis"
    assert length % MOBA_BLOCK == 0 and n_a >= 1 and n_b >= 1
    dh = k_norm.shape[0]
    cc, ss = _rope_tables(length, dh)

    depth = n_a + n_b
    g = s5_A_re.shape[1]
    w2, w3, w1, lam2 = _s5_prepare(s5_A_re, s5_A_im, s5_log_dt, s5_B_re, s5_B_im, s5_C_re,
                                   s5_C_im, min(g, V7X_LANES // (d // g)))
    a_norm3, d_skip3 = a_norm.reshape(n_a, 1, d), s5_D.reshape(n_a, 1, d)
    wglu, b_glu3 = w_glu.astype(BF16), b_glu.reshape(n_a, 1, 2 * d)
    fnorm3 = ffn_norm.reshape(depth, 1, d)
    wup, wdown = w_up.astype(BF16), w_down.astype(BF16)
    conv_b3 = conv_b.reshape(depth, 1, conv_b.shape[-1])
    wo = w_o.astype(BF16)

    h = x
    tm_a = min(512, rows)
    for i in range(n_a):
        h = _s5_layer(h, nbatch, length, d, i, i == 0, a_norm3, w2, lam2, w3, w1, d_skip3,
                      wglu, b_glu3)
        h = _ffn_layer(h.reshape(rows, d), rows, d, nbatch, rows // tm_a, tm_a, i, fnorm3, wup,
                       conv_w, conv_b3, wdown, batch_major_out=(i == n_a - 1))
        if i < n_a - 1:
            h = h.reshape(length // 2, 2, nbatch, d)

    h = h.reshape(rows, d)
    kaug, vt, kmean = _kv_proj(h, nbatch, length, d, kv_norm, w_kv, k_norm, cc, ss)
    kmean = kmean.transpose(0, 2, 1, 3)
    tm_b = min(512, length)
    for j in range(n_b):
        qaug = _q_proj(h, nbatch, length, d, b_norm[j], w_q[j], q_norm[j], cc, ss, kmean)
        o = _attention(qaug, kaug, vt)
        h = _ffn_layer(h, rows, d, 1, length // tm_b, tm_b, n_a + j, fnorm3, wup, conv_w,
                       conv_b3, wdown, attn=(o, wo, j))
    return h.reshape(nbatch, length, d)
```

```python
import functools

import jax
import jax.numpy as jnp
from jax import lax
from jax.experimental import pallas as pl
from jax.experimental.pallas import tpu as pltpu

MOBA_BLOCK = 256
MOBA_TOPK = 3
CONV_WIDTH = 3
ROPE_THETA = 10000.0
EPS = 1e-6

V7X_SUBLANES = 8
V7X_LANES = 128
V7X_MXU_DIM = 256
V7X_VMEM_BYTES = 64 * 1024 * 1024

MASK_BIAS = -1e30

F32 = jnp.float32
BF16 = jnp.bfloat16


def _vmem_limit(est_bytes):
    return int(min(V7X_VMEM_BYTES - 6 * 1024 * 1024, max(32 * 1024 * 1024, 2 * est_bytes)))


def _params(semantics, est_bytes):
    return pltpu.CompilerParams(dimension_semantics=semantics,
                                vmem_limit_bytes=_vmem_limit(est_bytes))


def _resident(shape):
    nd = len(shape)
    return pl.BlockSpec(shape, lambda *_: (0,) * nd, pipeline_mode=pl.Buffered(1))


def _layer_resident(arr, layer):
    nd = arr.ndim - 1
    return pl.BlockSpec((None,) + arr.shape[1:], lambda *_: (layer,) + (0,) * nd,
                        pipeline_mode=pl.Buffered(1))


def _rmsnorm(x, g):
    return x * lax.rsqrt(jnp.mean(x * x, axis=-1, keepdims=True) + EPS) * g


def _s5_kernel(*refs, n_slabs, slab_ch, nstate, nbatch, npairs, batch_major_in):
    n_in = n_slabs * slab_ch // V7X_LANES if batch_major_in else 1
    h_refs = refs[:n_in]
    (anorm_ref, w2_ref, lam2_ref, w3_ref, w1_ref, dskip_ref, wglu_ref, bglu_ref,
     out_ref, state_ref, z_ref, x_ref, xn_ref, pair_ref, g_ref, *xt) = refs[n_in:]
    half = npairs * nbatch
    d = x_ref.shape[-1]

    @pl.when(pl.program_id(0) == 0)
    def _():
        state_ref[...] = jnp.zeros_like(state_ref)

    if batch_major_in:
        xt_ref = xt[0]
        for s in range(n_in):
            for b in range(nbatch):
                for par in range(2):
                    xt_ref[s, pl.ds(par * half + b, npairs, stride=nbatch), :] = (
                        h_refs[s][b, pl.ds(par, npairs, stride=2), :])
            x_ref[:, s * V7X_LANES:(s + 1) * V7X_LANES] = xt_ref[s]
    else:
        h_ref = h_refs[0]
        x_ref[pl.ds(0, half), :] = h_ref[:, 0].reshape(half, d)
        x_ref[pl.ds(half, half), :] = h_ref[:, 1].reshape(half, d)

    xn = _rmsnorm(x_ref[...], anorm_ref[...])
    xn_ref[...] = xn
    xnb = xn.astype(BF16)

    for s in range(n_slabs):
        cs = slice(s * slab_ch, (s + 1) * slab_ch)
        pair_ref[s] = jnp.concatenate([xnb[:half, cs], xnb[half:, cs]], axis=1)
        zb = z_ref.at[s % 2]
        zb[...] = jnp.dot(pair_ref[s], w2_ref[s], preferred_element_type=F32)
        lr = jnp.broadcast_to(lam2_ref[s, 0:1, :], (nbatch, nstate))
        li = jnp.broadcast_to(lam2_ref[s, 1:2, :], (nbatch, nstate))
        sr = state_ref[s, :, pl.ds(0, nstate)]
        si = state_ref[s, :, pl.ds(nstate, nstate)]
        for k in range(npairs):
            rows = pl.ds(k * nbatch, nbatch)
            zr = zb[rows, pl.ds(0, nstate)]
            zi = zb[rows, pl.ds(nstate, nstate)]
            zb[rows, pl.ds(0, nstate)] = sr
            zb[rows, pl.ds(nstate, nstate)] = si
            sr, si = lr * sr - li * si + zr, lr * si + li * sr + zi
        state_ref[s, :, pl.ds(0, nstate)] = sr
        state_ref[s, :, pl.ds(nstate, nstate)] = si

        y = (jnp.dot(zb[...].astype(BF16), w3_ref[s], preferred_element_type=F32)
             + jnp.dot(pair_ref[s], w1_ref[s], preferred_element_type=F32))
        dsk = dskip_ref[:, cs]
        g_ref[pl.ds(0, half), cs] = jax.nn.gelu(
            y[:, :slab_ch] + dsk * xn_ref[pl.ds(0, half), cs]).astype(BF16)
        g_ref[pl.ds(half, half), cs] = jax.nn.gelu(
            y[:, slab_ch:] + dsk * xn_ref[pl.ds(half, half), cs]).astype(BF16)

    nc = V7X_MXU_DIM
    for c0 in range(0, d, nc):
        ca, cb = slice(c0, c0 + nc), slice(d + c0, d + c0 + nc)
        za = jnp.dot(g_ref[...], wglu_ref[:, ca], preferred_element_type=F32) + bglu_ref[:, ca]
        zb = jnp.dot(g_ref[...], wglu_ref[:, cb], preferred_element_type=F32) + bglu_ref[:, cb]
        res = x_ref[:, ca] + za * jax.nn.sigmoid(zb)
        out_ref[:, 0, :, ca] = res[:half].reshape(npairs, nbatch, nc)
        out_ref[:, 1, :, ca] = res[half:].reshape(npairs, nbatch, nc)


def _block_diag_lanes(src, n_blocks, row_block, col_block):
    rows = src.shape[-3]
    tiled = jnp.tile(src, (1,) * (src.ndim - 1) + (n_blocks,))
    g = (lax.broadcasted_iota(jnp.int32, (rows, n_blocks * col_block), 0) // row_block) % n_blocks
    h = lax.broadcasted_iota(jnp.int32, (rows, n_blocks * col_block), 1) // col_block
    out = jnp.where((g == h)[:, None, :], tiled, 0.0)
    return out.reshape(src.shape[:-2] + (src.shape[-2] * n_blocks * col_block,))


def _s5_prepare(a_re, a_im, log_dt, b_re, b_im, c_re, c_im, gs):
    na, g, p = a_re.shape
    c = b_re.shape[-1]
    ns = g // gs
    hi = lax.Precision.HIGHEST
    lam = lax.complex(a_re, a_im)
    dt = jnp.exp(log_dt)[..., None]
    lam_bar = jnp.exp(lam * dt)
    b_bar = ((lam_bar - 1.0) / lam)[..., None] * lax.complex(b_re, b_im)
    c_mat = lax.complex(c_re, c_im)
    lb = lam_bar[..., None] * b_bar
    cl1 = c_mat * lam_bar[:, :, None, :]
    cl2 = cl1 * lam_bar[:, :, None, :]
    lam2 = lam_bar * lam_bar

    def re_prod(cm, bm):
        return (jnp.einsum('lgdp,lgpc->lgdc', jnp.real(cm), jnp.real(bm), precision=hi)
                - jnp.einsum('lgdp,lgpc->lgdc', jnp.imag(cm), jnp.imag(bm), precision=hi))

    rcb = re_prod(c_mat, b_bar)
    rclb = re_prod(cl1, b_bar)

    s2 = jnp.stack([lb, b_bar], axis=1).reshape(na, 2, ns, gs, p, c)
    s2 = s2.transpose(0, 2, 1, 3, 5, 4)
    s2 = jnp.stack([jnp.real(s2), jnp.imag(s2)], axis=-2)
    w2 = _block_diag_lanes(s2.reshape(na, ns, 2 * gs * c, 2, p), gs, c, p)
    s3 = jnp.stack([cl1, cl2], axis=1).reshape(na, 2, ns, gs, c, p)
    s3 = s3.transpose(0, 2, 3, 5, 1, 4)
    s3 = jnp.stack([jnp.real(s3), -jnp.imag(s3)], axis=2)
    w3 = _block_diag_lanes(s3.reshape(na, ns, 2 * gs * p, 2, c), gs, p, c)
    zero = jnp.zeros_like(rcb)
    s1 = jnp.stack([jnp.stack([rcb, rclb], axis=1), jnp.stack([zero, rcb], axis=1)], axis=1)
    s1 = s1.reshape(na, 2, 2, ns, gs, c, c)
    s1 = s1.transpose(0, 3, 1, 4, 6, 2, 5)
    w1 = _block_diag_lanes(s1.reshape(na, ns, 2 * gs * c, 2, c), gs, c, c)
    lam2_t = jnp.stack([jnp.real(lam2).reshape(na, ns, gs * p),
                        jnp.imag(lam2).reshape(na, ns, gs * p)], axis=2)
    return w2.astype(BF16), w3.astype(BF16), w1.astype(BF16), lam2_t


def _s5_layer(h, nbatch, length, d, layer, batch_major_in, a_norm, w2, lam2, w3, w1, d_skip,
              wglu, b_glu):
    n_slabs, slab_ch = w2.shape[1], w2.shape[2] // 2
    nstate = w2.shape[3] // 2
    npairs = min(32, length // 2)
    half = npairs * nbatch
    tb = 2 * half
    if batch_major_in:
        n_in = d // V7X_LANES
        h_specs = [pl.BlockSpec((nbatch, 2 * npairs, V7X_LANES), functools.partial(
            lambda i, s: (0, i, s), s=s)) for s in range(n_in)]
        h_args = [h] * n_in
        extra = [pltpu.VMEM((n_in, tb, V7X_LANES), F32)]
    else:
        h_specs = [pl.BlockSpec((npairs, 2, nbatch, d), lambda i: (i, 0, 0, 0))]
        h_args = [h]
        extra = []
    est = (6 * tb * d * 4 + 2 * half * 2 * nstate * 4 + tb * d * 8 + 3 * tb * d * 2
           + n_slabs * slab_ch * 2 * (4 * nstate + 2 * slab_ch) * 2 + 2 * d * d * 2
           + 2 * tb * 2 * d * 4)
    kern = functools.partial(_s5_kernel, n_slabs=n_slabs, slab_ch=slab_ch, nstate=nstate,
                             nbatch=nbatch, npairs=npairs, batch_major_in=batch_major_in)
    return pl.pallas_call(
        kern,
        grid=(length // (2 * npairs),),
        in_specs=h_specs + [
            _layer_resident(a_norm, layer),
            _layer_resident(w2, layer),
            _layer_resident(lam2, layer),
            _layer_resident(w3, layer),
            _layer_resident(w1, layer),
            _layer_resident(d_skip, layer),
            _layer_resident(wglu, layer),
            _layer_resident(b_glu, layer),
        ],
        out_specs=pl.BlockSpec((npairs, 2, nbatch, d), lambda i: (i, 0, 0, 0)),
        out_shape=jax.ShapeDtypeStruct((length // 2, 2, nbatch, d), F32),
        scratch_shapes=[
            pltpu.VMEM((n_slabs, nbatch, 2 * nstate), F32),
            pltpu.VMEM((2, half, 2 * nstate), F32),
            pltpu.VMEM((tb, d), F32),
            pltpu.VMEM((tb, d), F32),
            pltpu.VMEM((n_slabs, half, 2 * slab_ch), BF16),
            pltpu.VMEM((tb, d), BF16),
        ] + extra,
        compiler_params=_params(("arbitrary",), est),
        name="s5_mixer",
    )(*h_args, a_norm, w2, lam2, w3, w1, d_skip, wglu, b_glu)


def _ffn_kernel(*refs, has_attn, shift, halo, tiles_per_seq, n_fc, fc, tm, n_slots,
                batch_major_out):
    if has_attn:
        (h_ref, o_ref, wo_ref, fnorm_ref, wup_ref, cw_ref, cb_ref, wdown_ref,
         out_ref, carry_ref, xnb_ref, act_ref, *hu) = refs
    else:
        (h_ref, fnorm_ref, wup_ref, cw_ref, cb_ref, wdown_ref,
         out_ref, carry_ref, xnb_ref, act_ref, *hu) = refs
    aligned = shift % V7X_SUBLANES == 0

    @pl.when(pl.program_id(0) % tiles_per_seq == 0)
    def _():
        carry_ref[...] = jnp.zeros_like(carry_ref)

    if batch_major_out:
        *hu, h1_ref, res_ref = hu
    else:
        h1_ref = out_ref
    h1 = h_ref[...]
    if has_attn:
        h1 = h1 + jnp.dot(o_ref[...], wo_ref[...], preferred_element_type=F32)
    h1_ref[...] = h1
    xnb_ref[...] = _rmsnorm(h1, fnorm_ref[...]).astype(BF16)

    def conv_branch(idx, slot):
        cols = slice(idx * fc, (idx + 1) * fc)
        u = jnp.dot(xnb_ref[...], wup_ref[:, cols], preferred_element_type=F32)
        w = cw_ref[:, cols]
        if aligned:
            prev = carry_ref[idx]
            carry_ref[idx] = u[tm - halo:, :]
            u1 = jnp.concatenate([prev[halo - shift:, :], u[:tm - shift, :]], axis=0)
            u2 = jnp.concatenate([prev[halo - 2 * shift:, :], u[:tm - 2 * shift, :]], axis=0)
        else:
            hu_ref = hu[0]
            hu_ref[slot, pl.ds(0, halo), :] = carry_ref[idx]
            hu_ref[slot, pl.ds(halo, tm), :] = u
            carry_ref[idx] = u[tm - halo:, :]
            u1 = hu_ref[slot, pl.ds(halo - shift, tm), :]
            u2 = hu_ref[slot, pl.ds(halo - 2 * shift, tm), :]
        return w[2:3, :] * u + w[1:2, :] * u1 + w[0:1, :] * u2 + cb_ref[:, cols]

    for c in range(n_fc):
        gate = conv_branch(c, (2 * c) % n_slots)
        val = conv_branch(n_fc + c, (2 * c + 1) % n_slots)
        act_ref[:, c * fc:(c + 1) * fc] = (jax.nn.silu(gate) * val).astype(BF16)

    res = h1_ref[...] + jnp.dot(act_ref[...], wdown_ref[...], preferred_element_type=F32)
    if batch_major_out:
        nbatch, tpb, d = out_ref.shape
        for s in range(d // V7X_LANES):
            res_ref[s] = res[:, s * V7X_LANES:(s + 1) * V7X_LANES]
        for b in range(nbatch):
            for s in range(d // V7X_LANES):
                out_ref[b, :, s * V7X_LANES:(s + 1) * V7X_LANES] = (
                    res_ref[s, pl.ds(b, tpb, stride=nbatch), :])
    else:
        out_ref[...] = res


def _ffn_layer(h, n_rows, d, shift, tiles_per_seq, tm, layer, ffn_norm, wup, conv_w,
               conv_b, wdown, attn=None, batch_major_out=False):
    f = wdown.shape[1]
    fc = V7X_MXU_DIM
    n_fc = f // fc
    halo = -(-2 * shift // V7X_SUBLANES) * V7X_SUBLANES
    aligned = shift % V7X_SUBLANES == 0
    n_slots = 4
    row_map = lambda i: (i, 0)
    in_specs = [pl.BlockSpec((tm, d), row_map)]
    args = [h]
    if attn is not None:
        o, wo, j = attn
        in_specs += [pl.BlockSpec((tm, d), row_map), _layer_resident(wo, j)]
        args += [o, wo]
    in_specs += [_layer_resident(ffn_norm, layer), _layer_resident(wup, layer),
                 _layer_resident(conv_w, layer), _layer_resident(conv_b, layer),
                 _layer_resident(wdown, layer)]
    args += [ffn_norm, wup, conv_w, conv_b, wdown]
    scratch = [
        pltpu.VMEM((2 * n_fc, halo, fc), F32),
        pltpu.VMEM((tm, d), BF16),
        pltpu.VMEM((tm, f), BF16),
    ]
    if not aligned:
        scratch.append(pltpu.VMEM((n_slots, tm + halo, fc), F32))
    if batch_major_out:
        scratch += [pltpu.VMEM((tm, d), F32), pltpu.VMEM((d // V7X_LANES, tm, V7X_LANES), F32)]
        out_spec = pl.BlockSpec((shift, tm // shift, d), lambda i: (0, i, 0))
        out_shape = jax.ShapeDtypeStruct((shift, n_rows // shift, d), F32)
    else:
        out_spec = pl.BlockSpec((tm, d), row_map)
        out_shape = jax.ShapeDtypeStruct((n_rows, d), F32)
    est = (d * 2 * f * 2 + f * d * 2 + d * d * 2 + 8 * tm * d * 4 + tm * d * 2 + tm * f * 2
           + n_slots * (tm + halo) * fc * 4 + 2 * n_fc * halo * fc * 4 + 8 * tm * fc * 4)
    kern = functools.partial(_ffn_kernel, has_attn=attn is not None, shift=shift, halo=halo,
                             tiles_per_seq=tiles_per_seq, n_fc=n_fc, fc=fc, tm=tm,
                             n_slots=n_slots, batch_major_out=batch_major_out)
    return pl.pallas_call(
        kern,
        grid=(n_rows // tm,),
        in_specs=in_specs,
        out_specs=out_spec,
        out_shape=out_shape,
        scratch_shapes=scratch,
        compiler_params=_params(("arbitrary",), est),
        name="conv_ffn_attn" if attn is not None else "conv_ffn",
    )(*args)


def _rope(x, cc, ss, dh):
    return x * cc + pltpu.roll(x, dh // 2, axis=1) * ss


def _kv_kernel(h_ref, nrm_ref, wk_ref, wvt_ref, knorm_ref, cc_ref, ss_ref,
               kaug_ref, vt_ref, kmean_ref, *, n_heads, dh):
    xn = _rmsnorm(h_ref[...], nrm_ref[...]).astype(BF16)
    kv = jnp.dot(xn, wk_ref[...], preferred_element_type=F32)
    vt = lax.dot_general(wvt_ref[...], xn, (((1,), (1,)), ((), ())),
                         preferred_element_type=F32)
    rows = kv.shape[0]
    blk = pl.program_id(1)
    lane = lax.broadcasted_iota(jnp.int32, (rows, dh), 1)
    onehot = jnp.where(lane == blk, 1.0, 0.0).astype(BF16)
    cc = cc_ref[...]
    ss = ss_ref[...]
    means = []
    for hd in range(n_heads):
        kh = _rmsnorm(kv[:, hd * dh:(hd + 1) * dh], knorm_ref[...])
        kr = _rope(kh, cc, ss, dh)
        kaug_ref[0, hd, :, pl.ds(0, dh)] = kr.astype(BF16)
        kaug_ref[0, hd, :, pl.ds(dh, dh)] = onehot
        vt_ref[0, hd] = vt[hd * dh:(hd + 1) * dh, :].astype(BF16)
        means.append(jnp.mean(kr, axis=0, keepdims=True))
    kmean_ref[0, 0] = jnp.concatenate(means, axis=0)


def _kv_proj(h, nbatch, length, d, kv_norm, w_kv, k_norm, cc, ss):
    dh = k_norm.shape[0]
    n_heads = w_kv.shape[1] // (2 * dh)
    nblk = length // MOBA_BLOCK
    t = MOBA_BLOCK
    est = w_kv.size * 2 + 2 * t * d * 4 + 3 * t * 2 * n_heads * dh * 4 + 6 * n_heads * t * dh * 2
    return pl.pallas_call(
        functools.partial(_kv_kernel, n_heads=n_heads, dh=dh),
        grid=(nbatch, nblk),
        in_specs=[
            pl.BlockSpec((t, d), lambda b, i: (b * nblk + i, 0)),
            _resident((1, d)),
            _resident((d, n_heads * dh)),
            _resident((n_heads * dh, d)),
            _resident((1, dh)),
            pl.BlockSpec((t, dh), lambda b, i: (i, 0)),
            pl.BlockSpec((t, dh), lambda b, i: (i, 0)),
        ],
        out_specs=[
            pl.BlockSpec((1, n_heads, t, 2 * dh), lambda b, i: (b, 0, i, 0)),
            pl.BlockSpec((1, n_heads, dh, t), lambda b, i: (b, 0, 0, i)),
            pl.BlockSpec((1, 1, n_heads, dh), lambda b, i: (b, i, 0, 0)),
        ],
        out_shape=[
            jax.ShapeDtypeStruct((nbatch, n_heads, length, 2 * dh), BF16),
            jax.ShapeDtypeStruct((nbatch, n_heads, dh, length), BF16),
            jax.ShapeDtypeStruct((nbatch, nblk, n_heads, dh), F32),
        ],
        compiler_params=_params(("parallel", "parallel"), est),
        name="kv_proj",
    )(h, kv_norm.reshape(1, d), w_kv[:, :n_heads * dh].astype(BF16),
      w_kv[:, n_heads * dh:].T.astype(BF16), k_norm.reshape(1, dh), cc, ss)


def _q_kernel(h_ref, nrm_ref, wq_ref, qnorm_ref, cc_ref, ss_ref, kmean_ref, qaug_ref,
              *, n_heads, dh, nblk, topk, scale):
    xn = _rmsnorm(h_ref[...], nrm_ref[...]).astype(BF16)
    q = jnp.dot(xn, wq_ref[...], preferred_element_type=F32)
    rows = q.shape[0]
    cur = pl.program_id(1)
    cc = cc_ref[...]
    ss = ss_ref[...]
    blk_id = lax.broadcasted_iota(jnp.int32, (nblk, rows), 0)
    for hd in range(n_heads):
        qh = _rmsnorm(q[:, hd * dh:(hd + 1) * dh], qnorm_ref[...])
        qr = _rope(qh, cc, ss, dh)
        gate_t = lax.dot_general(kmean_ref[0, hd], qr, (((1,), (1,)), ((), ())),
                                 precision=lax.Precision.HIGHEST,
                                 preferred_element_type=F32)
        gate_t = jnp.where(blk_id < cur, gate_t, -jnp.inf)
        sel_rows = []
        for j in range(nblk):
            gj = gate_t[j:j + 1, :]
            beats = jnp.where((gate_t > gj) | ((gate_t == gj) & (blk_id < j)), 1.0, 0.0)
            rank = jnp.sum(beats, axis=0, keepdims=True)
            sel_rows.append(
                jnp.where(((rank < topk) & (j < cur)) | (j == cur), 0.0, MASK_BIAS))
        bias_t = jnp.concatenate(
            sel_rows + [jnp.zeros((dh - nblk, rows), F32)], axis=0)
        qaug_ref[0, hd, :, pl.ds(0, dh)] = (qr * scale).astype(BF16)
        qaug_ref[0, hd, :, pl.ds(dh, dh)] = bias_t.T.astype(BF16)


def _q_proj(h, nbatch, length, d, b_norm, w_q, q_norm, cc, ss, kmean):
    dh = q_norm.shape[0]
    n_heads = w_q.shape[1] // dh
    nblk = length // MOBA_BLOCK
    t = MOBA_BLOCK
    est = w_q.size * 2 + 2 * t * d * 4 + 3 * t * n_heads * dh * 4 + 4 * n_heads * t * dh * 2
    kern = functools.partial(_q_kernel, n_heads=n_heads, dh=dh, nblk=nblk,
                             topk=min(MOBA_TOPK, nblk), scale=dh ** -0.5)
    return pl.pallas_call(
        kern,
        grid=(nbatch, nblk),
        in_specs=[
            pl.BlockSpec((t, d), lambda b, i: (b * nblk + i, 0)),
            _resident((1, d)),
            _resident(w_q.shape),
            _resident((1, dh)),
            pl.BlockSpec((t, dh), lambda b, i: (i, 0)),
            pl.BlockSpec((t, dh), lambda b, i: (i, 0)),
            pl.BlockSpec((1, n_heads, nblk, dh), lambda b, i: (b, 0, 0, 0)),
        ],
        out_specs=pl.BlockSpec((1, n_heads, t, 2 * dh), lambda b, i: (b, 0, i, 0)),
        out_shape=jax.ShapeDtypeStruct((nbatch, n_heads, length, 2 * dh), BF16),
        compiler_params=_params(("parallel", "parallel"), est),
        name="q_proj_gate",
    )(h, b_norm.reshape(1, d), w_q.astype(BF16), q_norm.reshape(1, dh), cc, ss, kmean)


def _attn_kernel(q_ref, k_ref, vt_ref, o_ref, s_ref, *, blk, nblk):
    nt = (((1,), (1,)), ((), ()))
    kpos = lax.broadcasted_iota(jnp.int32, (blk, blk), 0)
    qpos = lax.broadcasted_iota(jnp.int32, (blk, blk), 1)
    causal = kpos <= qpos

    def score_step(c, j, m_acc):
        st = lax.dot_general(k_ref[0, 0, j * blk:(j + 1) * blk, :],
                             q_ref[0, 0, c * blk:(c + 1) * blk, :], nt,
                             preferred_element_type=F32)
        if j == c:
            st = jnp.where(causal, st, -jnp.inf)
        s_ref[c % 4, j * blk:(j + 1) * blk, :] = st
        part = jnp.max(st.reshape(blk // V7X_SUBLANES, V7X_SUBLANES, blk), axis=0)
        return part if m_acc is None else jnp.maximum(m_acc, part)

    def value_step(c, j, m, l_acc, acc):
        pt = jnp.exp(s_ref[c % 4, j * blk:(j + 1) * blk, :] - m)
        pv = jnp.dot(vt_ref[0, 0, :, j * blk:(j + 1) * blk], pt.astype(BF16),
                     preferred_element_type=F32)
        part = jnp.sum(pt.reshape(blk // V7X_SUBLANES, V7X_SUBLANES, blk), axis=0)
        return (part if l_acc is None else l_acc + part), (pv if acc is None else acc + pv)

    def reduce_max(m_acc):
        return jnp.max(m_acc, axis=0, keepdims=True)

    ahead = min(3, nblk - 1)
    m_row = {}
    for c in range(ahead):
        m_acc = None
        for j in range(c + 1):
            m_acc = score_step(c, j, m_acc)
        m_row[c] = reduce_max(m_acc)
    for c in range(nblk):
        l_acc = acc = m_next = None
        nxt = c + ahead
        for j in range(max(c + 1, nxt + 1 if nxt < nblk else 0)):
            if nxt < nblk and j <= nxt:
                m_next = score_step(nxt, j, m_next)
            if j <= c:
                l_acc, acc = value_step(c, j, m_row[c], l_acc, acc)
        l = jnp.sum(l_acc, axis=0, keepdims=True)
        o_ref[c * blk:(c + 1) * blk, :] = (acc / l).T.astype(o_ref.dtype)
        if nxt < nblk:
            m_row[nxt] = reduce_max(m_next)


def _attention(qaug, kaug, vt):
    nbatch, n_heads, length, dh2 = qaug.shape
    dh = dh2 // 2
    blk = MOBA_BLOCK
    nblk = length // blk
    est = 4 * length * dh2 * 2 + 4 * length * dh * 2 + 4 * blk * length * 4
    return pl.pallas_call(
        functools.partial(_attn_kernel, blk=blk, nblk=nblk),
        grid=(nbatch, n_heads),
        in_specs=[
            pl.BlockSpec((1, 1, length, dh2), lambda b, h: (b, h, 0, 0)),
            pl.BlockSpec((1, 1, length, dh2), lambda b, h: (b, h, 0, 0)),
            pl.BlockSpec((1, 1, dh, length), lambda b, h: (b, h, 0, 0)),
        ],
        out_specs=pl.BlockSpec((length, dh), lambda b, h: (b, h)),
        out_shape=jax.ShapeDtypeStruct((nbatch * length, n_heads * dh), BF16),
        scratch_shapes=[pltpu.VMEM((4, length, blk), F32)],
        compiler_params=_params(("parallel", "parallel"), est),
        name="moba_attention",
    )(qaug, kaug, vt)


def _rope_tables(length, dh):
    half = dh // 2
    inv_freq = ROPE_THETA ** (-jnp.arange(half, dtype=F32) * 2.0 / dh)
    ang = jnp.arange(length, dtype=F32)[:, None] * inv_freq[None, :]
    cos, sin = jnp.cos(ang), jnp.sin(ang)
    return jnp.concatenate([cos, cos], axis=-1), jnp.concatenate([-sin, sin], axis=-1)


def kernel(x, a_norm, s5_A_re, s5_A_im, s5_log_dt, s5_B_re, s5_B_im, s5_C_re, s5_C_im, s5_D,
           w_glu, b_glu, kv_norm, w_kv, k_norm, b_norm, w_q, q_norm, w_o, ffn_norm, w_up,
           conv_w, conv_b, w_down):
    nbatch, length, d = x.shape
    n_a = a_norm.shape[0]
    n_b = b_norm.shape[0]
    rows = nbatch * length
    assert nbatch == V7X_SUBLANES, "the S5 recurrence keeps the batch on the sublane axis"
    assert length % MOBA_BLOCK == 0 and n_a >= 1 and n_b >= 1
    dh = k_norm.shape[0]
    cc, ss = _rope_tables(length, dh)

    depth = n_a + n_b
    g = s5_A_re.shape[1]
    w2, w3, w1, lam2 = _s5_prepare(s5_A_re, s5_A_im, s5_log_dt, s5_B_re, s5_B_im, s5_C_re,
                                   s5_C_im, min(g, V7X_LANES // (d // g)))
    a_norm3, d_skip3 = a_norm.reshape(n_a, 1, d), s5_D.reshape(n_a, 1, d)
    wglu, b_glu3 = w_glu.astype(BF16), b_glu.reshape(n_a, 1, 2 * d)
    fnorm3 = ffn_norm.reshape(depth, 1, d)
    wup, wdown = w_up.astype(BF16), w_down.astype(BF16)
    conv_b3 = conv_b.reshape(depth, 1, conv_b.shape[-1])
    wo = w_o.astype(BF16)

    h = x
    tm_a = min(1024, rows)
    for i in range(n_a):
        h = _s5_layer(h, nbatch, length, d, i, i == 0, a_norm3, w2, lam2, w3, w1, d_skip3,
                      wglu, b_glu3)
        h = _ffn_layer(h.reshape(rows, d), rows, d, nbatch, rows // tm_a, tm_a, i, fnorm3, wup,
                       conv_w, conv_b3, wdown, batch_major_out=(i == n_a - 1))
        if i < n_a - 1:
            h = h.reshape(length // 2, 2, nbatch, d)

    h = h.reshape(rows, d)
    kaug, vt, kmean = _kv_proj(h, nbatch, length, d, kv_norm, w_kv, k_norm, cc, ss)
    kmean = kmean.transpose(0, 2, 1, 3)
    tm_b = min(512, length)
    for j in range(n_b):
        qaug = _q_proj(h, nbatch, length, d, b_norm[j], w_q[j], q_norm[j], cc, ss, kmean)
        o = _attention(qaug, kaug, vt)
        h = _ffn_layer(h, rows, d, 1, length // tm_b, tm_b, n_a + j, fnorm3, wup, conv_w,
                       conv_b3, wdown, attn=(o, wo, j))
    return h.reshape(nbatch, length, d)
```

```python
import functools

import jax
import jax.numpy as jnp
from jax import lax
from jax.experimental import pallas as pl
from jax.experimental.pallas import tpu as pltpu

MOBA_BLOCK = 256
MOBA_TOPK = 3
CONV_WIDTH = 3
ROPE_THETA = 10000.0
EPS = 1e-6

V7X_SUBLANES = 8
V7X_LANES = 128
V7X_MXU_DIM = 256
V7X_VMEM_BYTES = 64 * 1024 * 1024

MASK_BIAS = -1e30
LOG2_E = 1.4426950408889634
V_EXTRA_ROWS = 16

F32 = jnp.float32
BF16 = jnp.bfloat16


def _vmem_limit(est_bytes):
    return int(min(V7X_VMEM_BYTES - 6 * 1024 * 1024, max(32 * 1024 * 1024, 2 * est_bytes)))


def _params(semantics, est_bytes):
    return pltpu.CompilerParams(dimension_semantics=semantics,
                                vmem_limit_bytes=_vmem_limit(est_bytes))


def _resident(shape):
    nd = len(shape)
    return pl.BlockSpec(shape, lambda *_: (0,) * nd, pipeline_mode=pl.Buffered(1))


def _layer_resident(arr, layer):
    nd = arr.ndim - 1
    return pl.BlockSpec((None,) + arr.shape[1:], lambda *_: (layer,) + (0,) * nd,
                        pipeline_mode=pl.Buffered(1))


def _rmsnorm(x, g):
    return x * lax.rsqrt(jnp.mean(x * x, axis=-1, keepdims=True) + EPS) * g


def _s5_kernel(*refs, n_slabs, slab_ch, nstate, nbatch, npairs, batch_major_in):
    n_in = n_slabs * slab_ch // V7X_LANES if batch_major_in else 1
    h_refs = refs[:n_in]
    (anorm_ref, w2_ref, lam2_ref, w3_ref, w1_ref, dskip_ref, wglu_ref, bglu_ref,
     out_ref, state_ref, z_ref, x_ref, xn_ref, pair_ref, g_ref, *xt) = refs[n_in:]
    half = npairs * nbatch
    d = x_ref.shape[-1]

    @pl.when(pl.program_id(0) == 0)
    def _():
        state_ref[...] = jnp.zeros_like(state_ref)

    if batch_major_in:
        xt_ref = xt[0]
        for s in range(n_in):
            for b in range(nbatch):
                for par in range(2):
                    xt_ref[s, pl.ds(par * half + b, npairs, stride=nbatch), :] = (
                        h_refs[s][b, pl.ds(par, npairs, stride=2), :])
            x_ref[:, s * V7X_LANES:(s + 1) * V7X_LANES] = xt_ref[s]
    else:
        h_ref = h_refs[0]
        x_ref[pl.ds(0, half), :] = h_ref[:, 0].reshape(half, d)
        x_ref[pl.ds(half, half), :] = h_ref[:, 1].reshape(half, d)

    xn = _rmsnorm(x_ref[...], anorm_ref[...])
    xn_ref[...] = xn
    xnb = xn.astype(BF16)

    for s in range(n_slabs):
        cs = slice(s * slab_ch, (s + 1) * slab_ch)
        pair_ref[s] = jnp.concatenate([xnb[:half, cs], xnb[half:, cs]], axis=1)
        zb = z_ref.at[s % 2]
        zb[...] = jnp.dot(pair_ref[s], w2_ref[s], preferred_element_type=F32)
        lr = jnp.broadcast_to(lam2_ref[s, 0:1, :], (nbatch, nstate))
        li = jnp.broadcast_to(lam2_ref[s, 1:2, :], (nbatch, nstate))
        sr = state_ref[s, :, pl.ds(0, nstate)]
        si = state_ref[s, :, pl.ds(nstate, nstate)]
        for k in range(npairs):
            rows = pl.ds(k * nbatch, nbatch)
            zr = zb[rows, pl.ds(0, nstate)]
            zi = zb[rows, pl.ds(nstate, nstate)]
            zb[rows, pl.ds(0, nstate)] = sr
            zb[rows, pl.ds(nstate, nstate)] = si
            sr, si = lr * sr - li * si + zr, lr * si + li * sr + zi
        state_ref[s, :, pl.ds(0, nstate)] = sr
        state_ref[s, :, pl.ds(nstate, nstate)] = si

        y = (jnp.dot(zb[...].astype(BF16), w3_ref[s], preferred_element_type=F32)
             + jnp.dot(pair_ref[s], w1_ref[s], preferred_element_type=F32))
        dsk = dskip_ref[:, cs]
        g_ref[pl.ds(0, half), cs] = jax.nn.gelu(
            y[:, :slab_ch] + dsk * xn_ref[pl.ds(0, half), cs]).astype(BF16)
        g_ref[pl.ds(half, half), cs] = jax.nn.gelu(
            y[:, slab_ch:] + dsk * xn_ref[pl.ds(half, half), cs]).astype(BF16)

    nc = V7X_MXU_DIM
    for c0 in range(0, d, nc):
        ca, cb = slice(c0, c0 + nc), slice(d + c0, d + c0 + nc)
        za = jnp.dot(g_ref[...], wglu_ref[:, ca], preferred_element_type=F32) + bglu_ref[:, ca]
        zb = jnp.dot(g_ref[...], wglu_ref[:, cb], preferred_element_type=F32) + bglu_ref[:, cb]
        res = x_ref[:, ca] + za * jax.nn.sigmoid(zb)
        out_ref[:, 0, :, ca] = res[:half].reshape(npairs, nbatch, nc)
        out_ref[:, 1, :, ca] = res[half:].reshape(npairs, nbatch, nc)


def _block_diag_lanes(src, n_blocks, row_block, col_block):
    rows = src.shape[-3]
    tiled = jnp.tile(src, (1,) * (src.ndim - 1) + (n_blocks,))
    g = (lax.broadcasted_iota(jnp.int32, (rows, n_blocks * col_block), 0) // row_block) % n_blocks
    h = lax.broadcasted_iota(jnp.int32, (rows, n_blocks * col_block), 1) // col_block
    out = jnp.where((g == h)[:, None, :], tiled, 0.0)
    return out.reshape(src.shape[:-2] + (src.shape[-2] * n_blocks * col_block,))


def _s5_prepare(a_re, a_im, log_dt, b_re, b_im, c_re, c_im, gs):
    na, g, p = a_re.shape
    c = b_re.shape[-1]
    ns = g // gs
    hi = lax.Precision.HIGHEST
    lam = lax.complex(a_re, a_im)
    dt = jnp.exp(log_dt)[..., None]
    lam_bar = jnp.exp(lam * dt)
    b_bar = ((lam_bar - 1.0) / lam)[..., None] * lax.complex(b_re, b_im)
    c_mat = lax.complex(c_re, c_im)
    lb = lam_bar[..., None] * b_bar
    cl1 = c_mat * lam_bar[:, :, None, :]
    cl2 = cl1 * lam_bar[:, :, None, :]
    lam2 = lam_bar * lam_bar

    def re_prod(cm, bm):
        return (jnp.einsum('lgdp,lgpc->lgdc', jnp.real(cm), jnp.real(bm), precision=hi)
                - jnp.einsum('lgdp,lgpc->lgdc', jnp.imag(cm), jnp.imag(bm), precision=hi))

    rcb = re_prod(c_mat, b_bar)
    rclb = re_prod(cl1, b_bar)

    s2 = jnp.stack([lb, b_bar], axis=1).reshape(na, 2, ns, gs, p, c)
    s2 = s2.transpose(0, 2, 1, 3, 5, 4)
    s2 = jnp.stack([jnp.real(s2), jnp.imag(s2)], axis=-2)
    w2 = _block_diag_lanes(s2.reshape(na, ns, 2 * gs * c, 2, p), gs, c, p)
    s3 = jnp.stack([cl1, cl2], axis=1).reshape(na, 2, ns, gs, c, p)
    s3 = s3.transpose(0, 2, 3, 5, 1, 4)
    s3 = jnp.stack([jnp.real(s3), -jnp.imag(s3)], axis=2)
    w3 = _block_diag_lanes(s3.reshape(na, ns, 2 * gs * p, 2, c), gs, p, c)
    zero = jnp.zeros_like(rcb)
    s1 = jnp.stack([jnp.stack([rcb, rclb], axis=1), jnp.stack([zero, rcb], axis=1)], axis=1)
    s1 = s1.reshape(na, 2, 2, ns, gs, c, c)
    s1 = s1.transpose(0, 3, 1, 4, 6, 2, 5)
    w1 = _block_diag_lanes(s1.reshape(na, ns, 2 * gs * c, 2, c), gs, c, c)
    lam2_t = jnp.stack([jnp.real(lam2).reshape(na, ns, gs * p),
                        jnp.imag(lam2).reshape(na, ns, gs * p)], axis=2)
    return w2.astype(BF16), w3.astype(BF16), w1.astype(BF16), lam2_t


def _s5_layer(h, nbatch, length, d, layer, batch_major_in, a_norm, w2, lam2, w3, w1, d_skip,
              wglu, b_glu):
    n_slabs, slab_ch = w2.shape[1], w2.shape[2] // 2
    nstate = w2.shape[3] // 2
    npairs = min(32, length // 2)
    half = npairs * nbatch
    tb = 2 * half
    if batch_major_in:
        n_in = d // V7X_LANES
        h_specs = [pl.BlockSpec((nbatch, 2 * npairs, V7X_LANES), functools.partial(
            lambda i, s: (0, i, s), s=s)) for s in range(n_in)]
        h_args = [h] * n_in
        extra = [pltpu.VMEM((n_in, tb, V7X_LANES), F32)]
    else:
        h_specs = [pl.BlockSpec((npairs, 2, nbatch, d), lambda i: (i, 0, 0, 0))]
        h_args = [h]
        extra = []
    est = (6 * tb * d * 4 + 2 * half * 2 * nstate * 4 + tb * d * 8 + 3 * tb * d * 2
           + n_slabs * slab_ch * 2 * (4 * nstate + 2 * slab_ch) * 2 + 2 * d * d * 2
           + 2 * tb * 2 * d * 4)
    kern = functools.partial(_s5_kernel, n_slabs=n_slabs, slab_ch=slab_ch, nstate=nstate,
                             nbatch=nbatch, npairs=npairs, batch_major_in=batch_major_in)
    return pl.pallas_call(
        kern,
        grid=(length // (2 * npairs),),
        in_specs=h_specs + [
            _layer_resident(a_norm, layer),
            _layer_resident(w2, layer),
            _layer_resident(lam2, layer),
            _layer_resident(w3, layer),
            _layer_resident(w1, layer),
            _layer_resident(d_skip, layer),
            _layer_resident(wglu, layer),
            _layer_resident(b_glu, layer),
        ],
        out_specs=pl.BlockSpec((npairs, 2, nbatch, d), lambda i: (i, 0, 0, 0)),
        out_shape=jax.ShapeDtypeStruct((length // 2, 2, nbatch, d), F32),
        scratch_shapes=[
            pltpu.VMEM((n_slabs, nbatch, 2 * nstate), F32),
            pltpu.VMEM((2, half, 2 * nstate), F32),
            pltpu.VMEM((tb, d), F32),
            pltpu.VMEM((tb, d), F32),
            pltpu.VMEM((n_slabs, half, 2 * slab_ch), BF16),
            pltpu.VMEM((tb, d), BF16),
        ] + extra,
        compiler_params=_params(("arbitrary",), est),
        name="s5_mixer",
    )(*h_args, a_norm, w2, lam2, w3, w1, d_skip, wglu, b_glu)


def _ffn_kernel(*refs, has_attn, shift, halo, tiles_per_seq, n_fc, fc, tm, n_slots,
                batch_major_out, cast_next):
    refs = list(refs)
    h_ref = refs.pop(0)
    o_ref, wo_ref = (refs.pop(0), refs.pop(0)) if has_attn else (None, None)
    fnorm_ref, wup_ref, cw_ref, cb_ref, wdown_ref = refs[:5]
    refs = refs[5:]
    if cast_next:
        wup_f32_ref, wdown_f32_ref, out_ref, wup_next_ref, wdown_next_ref = refs[:5]
        refs = refs[5:]
        wup_next_ref[...] = wup_f32_ref[...].astype(BF16)
        wdown_next_ref[...] = wdown_f32_ref[...].astype(BF16)
    else:
        out_ref = refs.pop(0)
    carry_ref, xnb_ref, act_ref, *hu = refs
    aligned = shift % V7X_SUBLANES == 0

    @pl.when(pl.program_id(0) % tiles_per_seq == 0)
    def _():
        carry_ref[...] = jnp.zeros_like(carry_ref)

    if batch_major_out:
        *hu, h1_ref, res_ref = hu
    else:
        h1_ref = out_ref
    h1 = h_ref[...]
    if has_attn:
        h1 = h1 + jnp.dot(o_ref[...], wo_ref[...], preferred_element_type=F32)
    h1_ref[...] = h1
    xnb_ref[...] = _rmsnorm(h1, fnorm_ref[...]).astype(BF16)

    def conv_branch(idx, slot):
        cols = slice(idx * fc, (idx + 1) * fc)
        u = jnp.dot(xnb_ref[...], wup_ref[:, cols], preferred_element_type=F32)
        w = cw_ref[:, cols]
        if aligned:
            prev = carry_ref[idx]
            carry_ref[idx] = u[tm - halo:, :]
            u1 = jnp.concatenate([prev[halo - shift:, :], u[:tm - shift, :]], axis=0)
            u2 = jnp.concatenate([prev[halo - 2 * shift:, :], u[:tm - 2 * shift, :]], axis=0)
        else:
            hu_ref = hu[0]
            hu_ref[slot, pl.ds(0, halo), :] = carry_ref[idx]
            hu_ref[slot, pl.ds(halo, tm), :] = u
            carry_ref[idx] = u[tm - halo:, :]
            u1 = hu_ref[slot, pl.ds(halo - shift, tm), :]
            u2 = hu_ref[slot, pl.ds(halo - 2 * shift, tm), :]
        return w[2:3, :] * u + w[1:2, :] * u1 + w[0:1, :] * u2 + cb_ref[:, cols]

    for c in range(n_fc):
        gate = conv_branch(c, (2 * c) % n_slots)
        val = conv_branch(n_fc + c, (2 * c + 1) % n_slots)
        act_ref[:, c * fc:(c + 1) * fc] = (jax.nn.silu(gate) * val).astype(BF16)

    res = h1_ref[...] + jnp.dot(act_ref[...], wdown_ref[...], preferred_element_type=F32)
    if batch_major_out:
        nbatch, tpb, d = out_ref.shape
        for s in range(d // V7X_LANES):
            res_ref[s] = res[:, s * V7X_LANES:(s + 1) * V7X_LANES]
        for b in range(nbatch):
            for s in range(d // V7X_LANES):
                out_ref[b, :, s * V7X_LANES:(s + 1) * V7X_LANES] = (
                    res_ref[s, pl.ds(b, tpb, stride=nbatch), :])
    else:
        out_ref[...] = res


def _cast_rows(total, n_steps):
    tile = 2 * V7X_SUBLANES
    return next(r for r in range(tile, total + 1, tile) if total % r == 0 and total // r <= n_steps)


def _ffn_layer(h, n_rows, d, shift, tiles_per_seq, tm, layer, ffn_norm, wup, conv_w,
               conv_b, wdown, attn=None, batch_major_out=False, next_f32=None):
    f = wdown.shape[0]
    fc = V7X_MXU_DIM
    n_fc = f // fc
    n_steps = n_rows // tm
    halo = -(-2 * shift // V7X_SUBLANES) * V7X_SUBLANES
    aligned = shift % V7X_SUBLANES == 0
    n_slots = 4
    row_map = lambda i: (i, 0)
    in_specs = [pl.BlockSpec((tm, d), row_map)]
    args = [h]
    if attn is not None:
        o, wo, j = attn
        in_specs += [pl.BlockSpec((tm, d), row_map), _layer_resident(wo, j)]
        args += [o, wo]
    in_specs += [_layer_resident(ffn_norm, layer), _resident(wup.shape),
                 _layer_resident(conv_w, layer), _layer_resident(conv_b, layer),
                 _resident(wdown.shape)]
    args += [ffn_norm, wup, conv_w, conv_b, wdown]
    scratch = [
        pltpu.VMEM((2 * n_fc, halo, fc), F32),
        pltpu.VMEM((tm, d), BF16),
        pltpu.VMEM((tm, f), BF16),
    ]
    if not aligned:
        scratch.append(pltpu.VMEM((n_slots, tm + halo, fc), F32))
    if batch_major_out:
        scratch += [pltpu.VMEM((tm, d), F32), pltpu.VMEM((d // V7X_LANES, tm, V7X_LANES), F32)]
        out_specs = [pl.BlockSpec((shift, tm // shift, d), lambda i: (0, i, 0))]
        out_shape = [jax.ShapeDtypeStruct((shift, n_rows // shift, d), F32)]
    else:
        out_specs = [pl.BlockSpec((tm, d), row_map)]
        out_shape = [jax.ShapeDtypeStruct((n_rows, d), F32)]
    if next_f32 is not None:
        for w in next_f32:
            rb = _cast_rows(w.shape[1], n_steps)
            last = w.shape[1] // rb - 1
            in_specs.append(pl.BlockSpec(
                (None, rb, w.shape[2]),
                functools.partial(lambda i, last: (layer + 1, jnp.minimum(i, last), 0), last=last)))
            args.append(w)
            out_specs.append(pl.BlockSpec(
                (rb, w.shape[2]),
                functools.partial(lambda i, last: (jnp.minimum(i, last), 0), last=last)))
            out_shape.append(jax.ShapeDtypeStruct(w.shape[1:], BF16))
    est = (d * 2 * f * 2 + f * d * 2 + d * d * 2 + 8 * tm * d * 4 + tm * d * 2 + tm * f * 2
           + n_slots * (tm + halo) * fc * 4 + 2 * n_fc * halo * fc * 4 + 8 * tm * fc * 4)
    kern = functools.partial(_ffn_kernel, has_attn=attn is not None, shift=shift, halo=halo,
                             tiles_per_seq=tiles_per_seq, n_fc=n_fc, fc=fc, tm=tm,
                             n_slots=n_slots, batch_major_out=batch_major_out,
                             cast_next=next_f32 is not None)
    outs = pl.pallas_call(
        kern,
        grid=(n_steps,),
        in_specs=in_specs,
        out_specs=out_specs,
        out_shape=out_shape,
        scratch_shapes=scratch,
        compiler_params=_params(("arbitrary",), est),
        name="conv_ffn_attn" if attn is not None else "conv_ffn",
    )(*args)
    return outs if next_f32 is not None else outs[0]


def _rope(x, cc, ss, dh):
    return x * cc + pltpu.roll(x, dh // 2, axis=1) * ss


def _kv_kernel(h_ref, nrm_ref, wk_ref, wvt_ref, knorm_ref, cc_ref, ss_ref,
               kaug_ref, vt_ref, kmean_ref, *, n_heads, dh):
    xn = _rmsnorm(h_ref[...], nrm_ref[...]).astype(BF16)
    kv = jnp.dot(xn, wk_ref[...], preferred_element_type=F32)
    vt = lax.dot_general(wvt_ref[...], xn, (((1,), (1,)), ((), ())),
                         preferred_element_type=F32)
    rows = kv.shape[0]
    blk = pl.program_id(1)
    lane = lax.broadcasted_iota(jnp.int32, (rows, dh), 1)
    onehot = jnp.where(lane == blk, 1.0, 0.0).astype(BF16)
    ones_row = jnp.where(lax.broadcasted_iota(jnp.int32, (V_EXTRA_ROWS, rows), 0) == 0,
                         1.0, 0.0).astype(BF16)
    cc = cc_ref[...]
    ss = ss_ref[...]
    means = []
    for hd in range(n_heads):
        kh = _rmsnorm(kv[:, hd * dh:(hd + 1) * dh], knorm_ref[...])
        kr = _rope(kh, cc, ss, dh)
        kaug_ref[0, hd, :, pl.ds(0, dh)] = kr.astype(BF16)
        kaug_ref[0, hd, :, pl.ds(dh, dh)] = onehot
        vt_ref[0, hd, pl.ds(0, dh), :] = vt[hd * dh:(hd + 1) * dh, :].astype(BF16)
        vt_ref[0, hd, pl.ds(dh, V_EXTRA_ROWS), :] = ones_row
        means.append(jnp.mean(kr, axis=0, keepdims=True))
    kmean_ref[0, 0] = jnp.concatenate(means, axis=0)


def _kv_proj(h, nbatch, length, d, kv_norm, w_kv, k_norm, cc, ss):
    dh = k_norm.shape[0]
    n_heads = w_kv.shape[1] // (2 * dh)
    nblk = length // MOBA_BLOCK
    t = MOBA_BLOCK
    est = w_kv.size * 2 + 2 * t * d * 4 + 3 * t * 2 * n_heads * dh * 4 + 6 * n_heads * t * dh * 2
    return pl.pallas_call(
        functools.partial(_kv_kernel, n_heads=n_heads, dh=dh),
        grid=(nbatch, nblk),
        in_specs=[
            pl.BlockSpec((t, d), lambda b, i: (b * nblk + i, 0)),
            _resident((1, d)),
            _resident((d, n_heads * dh)),
            _resident((n_heads * dh, d)),
            _resident((1, dh)),
            pl.BlockSpec((t, dh), lambda b, i: (i, 0)),
            pl.BlockSpec((t, dh), lambda b, i: (i, 0)),
        ],
        out_specs=[
            pl.BlockSpec((1, n_heads, t, 2 * dh), lambda b, i: (b, 0, i, 0)),
            pl.BlockSpec((1, n_heads, dh + V_EXTRA_ROWS, t), lambda b, i: (b, 0, 0, i)),
            pl.BlockSpec((1, 1, n_heads, dh), lambda b, i: (b, i, 0, 0)),
        ],
        out_shape=[
            jax.ShapeDtypeStruct((nbatch, n_heads, length, 2 * dh), BF16),
            jax.ShapeDtypeStruct((nbatch, n_heads, dh + V_EXTRA_ROWS, length), BF16),
            jax.ShapeDtypeStruct((nbatch, nblk, n_heads, dh), F32),
        ],
        compiler_params=_params(("parallel", "parallel"), est),
        name="kv_proj",
    )(h, kv_norm.reshape(1, d), w_kv[:, :n_heads * dh].astype(BF16),
      w_kv[:, n_heads * dh:].T.astype(BF16), k_norm.reshape(1, dh), cc, ss)


def _q_kernel(h_ref, nrm_ref, wq_ref, qnorm_ref, cc_ref, ss_ref, kmean_ref, qaug_ref,
              *, n_heads, dh, nblk, topk, scale):
    xn = _rmsnorm(h_ref[...], nrm_ref[...]).astype(BF16)
    q = jnp.dot(xn, wq_ref[...], preferred_element_type=F32)
    rows = q.shape[0]
    cur = pl.program_id(1)
    cc = cc_ref[...]
    ss = ss_ref[...]
    blk_id = lax.broadcasted_iota(jnp.int32, (nblk, rows), 0)
    for hd in range(n_heads):
        qh = _rmsnorm(q[:, hd * dh:(hd + 1) * dh], qnorm_ref[...])
        qr = _rope(qh, cc, ss, dh)
        gate_t = lax.dot_general(kmean_ref[0, hd], qr, (((1,), (1,)), ((), ())),
                                 precision=lax.Precision.HIGHEST,
                                 preferred_element_type=F32)
        gate_t = jnp.where(blk_id < cur, gate_t, -jnp.inf)
        sel_rows = []
        for j in range(nblk):
            gj = gate_t[j:j + 1, :]
            beats = jnp.where((gate_t > gj) | ((gate_t == gj) & (blk_id < j)), 1.0, 0.0)
            rank = jnp.sum(beats, axis=0, keepdims=True)
            sel_rows.append(
                jnp.where(((rank < topk) & (j < cur)) | (j == cur), 0.0, MASK_BIAS))
        bias_t = jnp.concatenate(
            sel_rows + [jnp.zeros((dh - nblk, rows), F32)], axis=0)
        qaug_ref[0, hd, :, pl.ds(0, dh)] = (qr * scale).astype(BF16)
        qaug_ref[0, hd, :, pl.ds(dh, dh)] = bias_t.T.astype(BF16)


def _q_proj(h, nbatch, length, d, b_norm, w_q, q_norm, cc, ss, kmean):
    dh = q_norm.shape[0]
    n_heads = w_q.shape[1] // dh
    nblk = length // MOBA_BLOCK
    t = MOBA_BLOCK
    est = w_q.size * 2 + 2 * t * d * 4 + 3 * t * n_heads * dh * 4 + 4 * n_heads * t * dh * 2
    kern = functools.partial(_q_kernel, n_heads=n_heads, dh=dh, nblk=nblk,
                             topk=min(MOBA_TOPK, nblk), scale=dh ** -0.5 * LOG2_E)
    return pl.pallas_call(
        kern,
        grid=(nbatch, nblk),
        in_specs=[
            pl.BlockSpec((t, d), lambda b, i: (b * nblk + i, 0)),
            _resident((1, d)),
            _resident(w_q.shape),
            _resident((1, dh)),
            pl.BlockSpec((t, dh), lambda b, i: (i, 0)),
            pl.BlockSpec((t, dh), lambda b, i: (i, 0)),
            pl.BlockSpec((1, n_heads, nblk, dh), lambda b, i: (b, 0, 0, 0)),
        ],
        out_specs=pl.BlockSpec((1, n_heads, t, 2 * dh), lambda b, i: (b, 0, i, 0)),
        out_shape=jax.ShapeDtypeStruct((nbatch, n_heads, length, 2 * dh), BF16),
        compiler_params=_params(("parallel", "parallel"), est),
        name="q_proj_gate",
    )(h, b_norm.reshape(1, d), w_q.astype(BF16), q_norm.reshape(1, dh), cc, ss, kmean)


def _attn_kernel(q_ref, k_ref, vt_ref, o_ref, s_ref, *, blk, nblk, dh):
    nt = (((1,), (1,)), ((), ()))
    kpos = lax.broadcasted_iota(jnp.int32, (blk, blk), 0)
    qpos = lax.broadcasted_iota(jnp.int32, (blk, blk), 1)
    causal = kpos <= qpos

    def score_step(c, j, m_acc):
        st = lax.dot_general(k_ref[0, 0, j * blk:(j + 1) * blk, :],
                             q_ref[0, 0, c * blk:(c + 1) * blk, :], nt,
                             preferred_element_type=F32)
        if j == c:
            st = jnp.where(causal, st, -jnp.inf)
        s_ref[c % 4, j * blk:(j + 1) * blk, :] = st
        part = jnp.max(st.reshape(blk // V7X_SUBLANES, V7X_SUBLANES, blk), axis=0)
        return part if m_acc is None else jnp.maximum(m_acc, part)

    def value_step(c, j, m, acc):
        pt = jnp.exp2(s_ref[c % 4, j * blk:(j + 1) * blk, :] - m)
        pv = jnp.dot(vt_ref[0, 0, :, j * blk:(j + 1) * blk], pt.astype(BF16),
                     preferred_element_type=F32)
        return pv if acc is None else acc + pv

    def reduce_max(m_acc):
        return jnp.max(m_acc, axis=0, keepdims=True)

    ahead = min(3, nblk - 1)
    m_row = {}
    for c in range(ahead):
        m_acc = None
        for j in range(c + 1):
            m_acc = score_step(c, j, m_acc)
        m_row[c] = reduce_max(m_acc)
    for c in range(nblk):
        acc = m_next = None
        nxt = c + ahead
        for j in range(max(c + 1, nxt + 1 if nxt < nblk else 0)):
            if nxt < nblk and j <= nxt:
                m_next = score_step(nxt, j, m_next)
            if j <= c:
                acc = value_step(c, j, m_row[c], acc)
        o_ref[c * blk:(c + 1) * blk, :] = (acc[:dh] / acc[dh:dh + 1]).T.astype(o_ref.dtype)
        if nxt < nblk:
            m_row[nxt] = reduce_max(m_next)


def _attention(qaug, kaug, vt):
    nbatch, n_heads, length, dh2 = qaug.shape
    dh = dh2 // 2
    blk = MOBA_BLOCK
    nblk = length // blk
    est = 4 * length * dh2 * 2 + 4 * length * dh * 2 + 4 * blk * length * 4
    return pl.pallas_call(
        functools.partial(_attn_kernel, blk=blk, nblk=nblk, dh=dh),
        grid=(nbatch, n_heads),
        in_specs=[
            pl.BlockSpec((1, 1, length, dh2), lambda b, h: (b, h, 0, 0)),
            pl.BlockSpec((1, 1, length, dh2), lambda b, h: (b, h, 0, 0)),
            pl.BlockSpec((1, 1, vt.shape[2], length), lambda b, h: (b, h, 0, 0)),
        ],
        out_specs=pl.BlockSpec((length, dh), lambda b, h: (b, h)),
        out_shape=jax.ShapeDtypeStruct((nbatch * length, n_heads * dh), BF16),
        scratch_shapes=[pltpu.VMEM((4, length, blk), F32)],
        compiler_params=_params(("parallel", "parallel"), est),
        name="moba_attention",
    )(qaug, kaug, vt)


def _rope_tables(length, dh):
    half = dh // 2
    inv_freq = ROPE_THETA ** (-jnp.arange(half, dtype=F32) * 2.0 / dh)
    ang = jnp.arange(length, dtype=F32)[:, None] * inv_freq[None, :]
    cos, sin = jnp.cos(ang), jnp.sin(ang)
    return jnp.concatenate([cos, cos], axis=-1), jnp.concatenate([-sin, sin], axis=-1)


def kernel(x, a_norm, s5_A_re, s5_A_im, s5_log_dt, s5_B_re, s5_B_im, s5_C_re, s5_C_im, s5_D,
           w_glu, b_glu, kv_norm, w_kv, k_norm, b_norm, w_q, q_norm, w_o, ffn_norm, w_up,
           conv_w, conv_b, w_down):
    nbatch, length, d = x.shape
    n_a = a_norm.shape[0]
    n_b = b_norm.shape[0]
    rows = nbatch * length
    assert nbatch == V7X_SUBLANES, "the S5 recurrence keeps the batch on the sublane axis"
    assert length % MOBA_BLOCK == 0 and n_a >= 1 and n_b >= 1
    dh = k_norm.shape[0]
    cc, ss = _rope_tables(length, dh)

    depth = n_a + n_b
    g = s5_A_re.shape[1]
    w2, w3, w1, lam2 = _s5_prepare(s5_A_re, s5_A_im, s5_log_dt, s5_B_re, s5_B_im, s5_C_re,
                                   s5_C_im, min(g, V7X_LANES // (d // g)))
    a_norm3, d_skip3 = a_norm.reshape(n_a, 1, d), s5_D.reshape(n_a, 1, d)
    wglu, b_glu3 = w_glu.astype(BF16), b_glu.reshape(n_a, 1, 2 * d)
    fnorm3 = ffn_norm.reshape(depth, 1, d)
    wup, wdown = w_up[0].astype(BF16), w_down[0].astype(BF16)
    conv_b3 = conv_b.reshape(depth, 1, conv_b.shape[-1])
    wo = w_o.astype(BF16)

    h = x
    tm_a = min(1024, rows)
    for i in range(n_a):
        h = _s5_layer(h, nbatch, length, d, i, i == 0, a_norm3, w2, lam2, w3, w1, d_skip3,
                      wglu, b_glu3)
        h, wup, wdown = _ffn_layer(h.reshape(rows, d), rows, d, nbatch, rows // tm_a, tm_a, i,
                                   fnorm3, wup, conv_w, conv_b3, wdown,
                                   batch_major_out=(i == n_a - 1), next_f32=(w_up, w_down))
        if i < n_a - 1:
            h = h.reshape(length // 2, 2, nbatch, d)

    h = h.reshape(rows, d)
    kaug, vt, kmean = _kv_proj(h, nbatch, length, d, kv_norm, w_kv, k_norm, cc, ss)
    kmean = kmean.transpose(0, 2, 1, 3)
    tm_b = min(512, length)
    for j in range(n_b):
        qaug = _q_proj(h, nbatch, length, d, b_norm[j], w_q[j], q_norm[j], cc, ss, kmean)
        o = _attention(qaug, kaug, vt)
        last = j == n_b - 1
        res = _ffn_layer(h, rows, d, 1, length // tm_b, tm_b, n_a + j, fnorm3, wup, conv_w,
                         conv_b3, wdown, attn=(o, wo, j),
                         next_f32=None if last else (w_up, w_down))
        h, wup, wdown = (res, None, None) if last else res
    return h.reshape(nbatch, length, d)
```

```python
import functools

import jax
import jax.numpy as jnp
from jax import lax
from jax.experimental import pallas as pl
from jax.experimental.pallas import tpu as pltpu

MOBA_BLOCK = 256
MOBA_TOPK = 3
CONV_WIDTH = 3
ROPE_THETA = 10000.0
EPS = 1e-6

V7X_SUBLANES = 8
V7X_LANES = 128
V7X_MXU_DIM = 256
V7X_VMEM_BYTES = 64 * 1024 * 1024

MASK_BIAS = -1e30
LOG2_E = 1.4426950408889634
V_EXTRA_ROWS = 16

F32 = jnp.float32
BF16 = jnp.bfloat16


def _vmem_limit(est_bytes):
    return int(min(V7X_VMEM_BYTES - 6 * 1024 * 1024, max(32 * 1024 * 1024, 2 * est_bytes)))


def _params(semantics, est_bytes):
    return pltpu.CompilerParams(dimension_semantics=semantics,
                                vmem_limit_bytes=_vmem_limit(est_bytes))


def _resident(shape):
    nd = len(shape)
    return pl.BlockSpec(shape, lambda *_: (0,) * nd, pipeline_mode=pl.Buffered(1))


def _layer_resident(arr, layer):
    nd = arr.ndim - 1
    return pl.BlockSpec((None,) + arr.shape[1:], lambda *_: (layer,) + (0,) * nd,
                        pipeline_mode=pl.Buffered(1))


def _cast_rows(total, n_steps):
    tile = 2 * V7X_SUBLANES
    return next(r for r in range(tile, total + 1, tile) if total % r == 0 and total // r <= n_steps)


def _rmsnorm(x, g):
    return x * lax.rsqrt(jnp.mean(x * x, axis=-1, keepdims=True) + EPS) * g


def _s5_kernel(*refs, n_slabs, slab_ch, nstate, nbatch, npairs, batch_major_in, cast_next):
    n_in = n_slabs * slab_ch // V7X_LANES if batch_major_in else 1
    h_refs = refs[:n_in]
    (anorm_ref, w2_ref, lam2_ref, w3_ref, w1_ref, dskip_ref, wglu_ref, bglu_ref,
     *refs) = refs[n_in:]
    if cast_next:
        wup_f32_ref, wdown_f32_ref, out_ref, wup_next_ref, wdown_next_ref, *refs = refs
        wup_next_ref[...] = wup_f32_ref[...].astype(BF16)
        wdown_next_ref[...] = wdown_f32_ref[...].astype(BF16)
    else:
        out_ref, *refs = refs
    state_ref, z_ref, x_ref, xn_ref, pair_ref, g_ref, *xt = refs
    half = npairs * nbatch
    d = x_ref.shape[-1]

    @pl.when(pl.program_id(0) == 0)
    def _():
        state_ref[...] = jnp.zeros_like(state_ref)

    if batch_major_in:
        xt_ref = xt[0]
        for s in range(n_in):
            for b in range(nbatch):
                for par in range(2):
                    xt_ref[s, pl.ds(par * half + b, npairs, stride=nbatch), :] = (
                        h_refs[s][b, pl.ds(par, npairs, stride=2), :])
            x_ref[:, s * V7X_LANES:(s + 1) * V7X_LANES] = xt_ref[s]
    else:
        h_ref = h_refs[0]
        x_ref[pl.ds(0, half), :] = h_ref[:, 0].reshape(half, d)
        x_ref[pl.ds(half, half), :] = h_ref[:, 1].reshape(half, d)

    xn = _rmsnorm(x_ref[...], anorm_ref[...])
    xn_ref[...] = xn
    xnb = xn.astype(BF16)

    for s in range(n_slabs):
        cs = slice(s * slab_ch, (s + 1) * slab_ch)
        pair_ref[s] = jnp.concatenate([xnb[:half, cs], xnb[half:, cs]], axis=1)
        zb = z_ref.at[s % 2]
        zb[...] = jnp.dot(pair_ref[s], w2_ref[s], preferred_element_type=F32)
        lr = jnp.broadcast_to(lam2_ref[s, 0:1, :], (nbatch, nstate))
        li = jnp.broadcast_to(lam2_ref[s, 1:2, :], (nbatch, nstate))
        sr = state_ref[s, :, pl.ds(0, nstate)]
        si = state_ref[s, :, pl.ds(nstate, nstate)]
        for k in range(npairs):
            rows = pl.ds(k * nbatch, nbatch)
            zr = zb[rows, pl.ds(0, nstate)]
            zi = zb[rows, pl.ds(nstate, nstate)]
            zb[rows, pl.ds(0, nstate)] = sr
            zb[rows, pl.ds(nstate, nstate)] = si
            sr, si = lr * sr - li * si + zr, lr * si + li * sr + zi
        state_ref[s, :, pl.ds(0, nstate)] = sr
        state_ref[s, :, pl.ds(nstate, nstate)] = si

        y = (jnp.dot(zb[...].astype(BF16), w3_ref[s], preferred_element_type=F32)
             + jnp.dot(pair_ref[s], w1_ref[s], preferred_element_type=F32))
        dsk = dskip_ref[:, cs]
        g_ref[pl.ds(0, half), cs] = jax.nn.gelu(
            y[:, :slab_ch] + dsk * xn_ref[pl.ds(0, half), cs]).astype(BF16)
        g_ref[pl.ds(half, half), cs] = jax.nn.gelu(
            y[:, slab_ch:] + dsk * xn_ref[pl.ds(half, half), cs]).astype(BF16)

    nc = V7X_MXU_DIM
    for c0 in range(0, d, nc):
        ca, cb = slice(c0, c0 + nc), slice(d + c0, d + c0 + nc)
        za = jnp.dot(g_ref[...], wglu_ref[:, ca], preferred_element_type=F32) + bglu_ref[:, ca]
        zb = jnp.dot(g_ref[...], wglu_ref[:, cb], preferred_element_type=F32) + bglu_ref[:, cb]
        res = x_ref[:, ca] + za * jax.nn.sigmoid(zb)
        out_ref[:, 0, :, ca] = res[:half].reshape(npairs, nbatch, nc)
        out_ref[:, 1, :, ca] = res[half:].reshape(npairs, nbatch, nc)


def _block_diag_lanes(src, n_blocks, row_block, col_block):
    rows = src.shape[-3]
    tiled = jnp.tile(src, (1,) * (src.ndim - 1) + (n_blocks,))
    g = (lax.broadcasted_iota(jnp.int32, (rows, n_blocks * col_block), 0) // row_block) % n_blocks
    h = lax.broadcasted_iota(jnp.int32, (rows, n_blocks * col_block), 1) // col_block
    out = jnp.where((g == h)[:, None, :], tiled, 0.0)
    return out.reshape(src.shape[:-2] + (src.shape[-2] * n_blocks * col_block,))


def _s5_prepare(a_re, a_im, log_dt, b_re, b_im, c_re, c_im, gs):
    na, g, p = a_re.shape
    c = b_re.shape[-1]
    ns = g // gs
    hi = lax.Precision.HIGHEST
    lam = lax.complex(a_re, a_im)
    dt = jnp.exp(log_dt)[..., None]
    lam_bar = jnp.exp(lam * dt)
    b_bar = ((lam_bar - 1.0) / lam)[..., None] * lax.complex(b_re, b_im)
    c_mat = lax.complex(c_re, c_im)
    lb = lam_bar[..., None] * b_bar
    cl1 = c_mat * lam_bar[:, :, None, :]
    cl2 = cl1 * lam_bar[:, :, None, :]
    lam2 = lam_bar * lam_bar

    def re_prod(cm, bm):
        return (jnp.einsum('lgdp,lgpc->lgdc', jnp.real(cm), jnp.real(bm), precision=hi)
                - jnp.einsum('lgdp,lgpc->lgdc', jnp.imag(cm), jnp.imag(bm), precision=hi))

    rcb = re_prod(c_mat, b_bar)
    rclb = re_prod(cl1, b_bar)

    s2 = jnp.stack([lb, b_bar], axis=1).reshape(na, 2, ns, gs, p, c)
    s2 = s2.transpose(0, 2, 1, 3, 5, 4)
    s2 = jnp.stack([jnp.real(s2), jnp.imag(s2)], axis=-2)
    w2 = _block_diag_lanes(s2.reshape(na, ns, 2 * gs * c, 2, p), gs, c, p)
    s3 = jnp.stack([cl1, cl2], axis=1).reshape(na, 2, ns, gs, c, p)
    s3 = s3.transpose(0, 2, 3, 5, 1, 4)
    s3 = jnp.stack([jnp.real(s3), -jnp.imag(s3)], axis=2)
    w3 = _block_diag_lanes(s3.reshape(na, ns, 2 * gs * p, 2, c), gs, p, c)
    zero = jnp.zeros_like(rcb)
    s1 = jnp.stack([jnp.stack([rcb, rclb], axis=1), jnp.stack([zero, rcb], axis=1)], axis=1)
    s1 = s1.reshape(na, 2, 2, ns, gs, c, c)
    s1 = s1.transpose(0, 3, 1, 4, 6, 2, 5)
    w1 = _block_diag_lanes(s1.reshape(na, ns, 2 * gs * c, 2, c), gs, c, c)
    lam2_t = jnp.stack([jnp.real(lam2).reshape(na, ns, gs * p),
                        jnp.imag(lam2).reshape(na, ns, gs * p)], axis=2)
    return w2.astype(BF16), w3.astype(BF16), w1.astype(BF16), lam2_t


def _s5_layer(h, nbatch, length, d, layer, batch_major_in, a_norm, w2, lam2, w3, w1, d_skip,
              wglu, b_glu, next_f32=None):
    n_slabs, slab_ch = w2.shape[1], w2.shape[2] // 2
    nstate = w2.shape[3] // 2
    npairs = min(32, length // 2)
    half = npairs * nbatch
    tb = 2 * half
    if batch_major_in:
        n_in = d // V7X_LANES
        h_specs = [pl.BlockSpec((nbatch, 2 * npairs, V7X_LANES), functools.partial(
            lambda i, s: (0, i, s), s=s)) for s in range(n_in)]
        h_args = [h] * n_in
        extra = [pltpu.VMEM((n_in, tb, V7X_LANES), F32)]
    else:
        h_specs = [pl.BlockSpec((npairs, 2, nbatch, d), lambda i: (i, 0, 0, 0))]
        h_args = [h]
        extra = []
    est = (6 * tb * d * 4 + 2 * half * 2 * nstate * 4 + tb * d * 8 + 3 * tb * d * 2
           + n_slabs * slab_ch * 2 * (4 * nstate + 2 * slab_ch) * 2 + 2 * d * d * 2
           + 2 * tb * 2 * d * 4)
    kern = functools.partial(_s5_kernel, n_slabs=n_slabs, slab_ch=slab_ch, nstate=nstate,
                             nbatch=nbatch, npairs=npairs, batch_major_in=batch_major_in,
                             cast_next=next_f32 is not None)
    n_steps = length // (2 * npairs)
    cast_specs, cast_args = [], []
    out_specs = [pl.BlockSpec((npairs, 2, nbatch, d), lambda i: (i, 0, 0, 0))]
    out_shape = [jax.ShapeDtypeStruct((length // 2, 2, nbatch, d), F32)]
    for w in next_f32 or ():
        rb = _cast_rows(w.shape[1], n_steps)
        last = w.shape[1] // rb - 1
        cast_specs.append(pl.BlockSpec(
            (None, rb, w.shape[2]),
            functools.partial(lambda i, last: (layer, jnp.minimum(i, last), 0), last=last)))
        cast_args.append(w)
        out_specs.append(pl.BlockSpec(
            (rb, w.shape[2]),
            functools.partial(lambda i, last: (jnp.minimum(i, last), 0), last=last)))
        out_shape.append(jax.ShapeDtypeStruct(w.shape[1:], BF16))
    outs = pl.pallas_call(
        kern,
        grid=(n_steps,),
        in_specs=h_specs + [
            _layer_resident(a_norm, layer),
            _layer_resident(w2, layer),
            _layer_resident(lam2, layer),
            _layer_resident(w3, layer),
            _layer_resident(w1, layer),
            _layer_resident(d_skip, layer),
            _layer_resident(wglu, layer),
            _layer_resident(b_glu, layer),
        ] + cast_specs,
        out_specs=out_specs,
        out_shape=out_shape,
        scratch_shapes=[
            pltpu.VMEM((n_slabs, nbatch, 2 * nstate), F32),
            pltpu.VMEM((2, half, 2 * nstate), F32),
            pltpu.VMEM((tb, d), F32),
            pltpu.VMEM((tb, d), F32),
            pltpu.VMEM((n_slabs, half, 2 * slab_ch), BF16),
            pltpu.VMEM((tb, d), BF16),
        ] + extra,
        compiler_params=_params(("arbitrary",), est),
        name="s5_mixer",
    )(*h_args, a_norm, w2, lam2, w3, w1, d_skip, wglu, b_glu, *cast_args)
    return outs if next_f32 is not None else outs[0]


def _ffn_kernel(*refs, has_attn, shift, halo, tiles_per_seq, n_fc, fc, tm, n_slots,
                batch_major_out, cast_next):
    refs = list(refs)
    h_ref = refs.pop(0)
    o_ref, wo_ref = (refs.pop(0), refs.pop(0)) if has_attn else (None, None)
    fnorm_ref, wup_ref, cw_ref, cb_ref, wdown_ref = refs[:5]
    refs = refs[5:]
    if cast_next:
        wup_f32_ref, wdown_f32_ref, out_ref, wup_next_ref, wdown_next_ref = refs[:5]
        refs = refs[5:]
        wup_next_ref[...] = wup_f32_ref[...].astype(BF16)
        wdown_next_ref[...] = wdown_f32_ref[...].astype(BF16)
    else:
        out_ref = refs.pop(0)
    carry_ref, xnb_ref, act_ref, *hu = refs
    aligned = shift % V7X_SUBLANES == 0

    @pl.when(pl.program_id(0) % tiles_per_seq == 0)
    def _():
        carry_ref[...] = jnp.zeros_like(carry_ref)

    if batch_major_out:
        *hu, h1_ref, res_ref = hu
    else:
        h1_ref = out_ref
    h1 = h_ref[...]
    if has_attn:
        h1 = h1 + jnp.dot(o_ref[...], wo_ref[...], preferred_element_type=F32)
    h1_ref[...] = h1
    xnb_ref[...] = _rmsnorm(h1, fnorm_ref[...]).astype(BF16)

    def conv_branch(idx, slot):
        cols = slice(idx * fc, (idx + 1) * fc)
        u = jnp.dot(xnb_ref[...], wup_ref[:, cols], preferred_element_type=F32)
        w = cw_ref[:, cols]
        if aligned:
            prev = carry_ref[idx]
            carry_ref[idx] = u[tm - halo:, :]
            u1 = jnp.concatenate([prev[halo - shift:, :], u[:tm - shift, :]], axis=0)
            u2 = jnp.concatenate([prev[halo - 2 * shift:, :], u[:tm - 2 * shift, :]], axis=0)
        else:
            hu_ref = hu[0]
            hu_ref[slot, pl.ds(0, halo), :] = carry_ref[idx]
            hu_ref[slot, pl.ds(halo, tm), :] = u
            carry_ref[idx] = u[tm - halo:, :]
            u1 = hu_ref[slot, pl.ds(halo - shift, tm), :]
            u2 = hu_ref[slot, pl.ds(halo - 2 * shift, tm), :]
        return w[2:3, :] * u + w[1:2, :] * u1 + w[0:1, :] * u2 + cb_ref[:, cols]

    for c in range(n_fc):
        gate = conv_branch(c, (2 * c) % n_slots)
        val = conv_branch(n_fc + c, (2 * c + 1) % n_slots)
        act_ref[:, c * fc:(c + 1) * fc] = (jax.nn.silu(gate) * val).astype(BF16)

    res = h1_ref[...] + jnp.dot(act_ref[...], wdown_ref[...], preferred_element_type=F32)
    if batch_major_out:
        nbatch, tpb, d = out_ref.shape
        for s in range(d // V7X_LANES):
            res_ref[s] = res[:, s * V7X_LANES:(s + 1) * V7X_LANES]
        for b in range(nbatch):
            for s in range(d // V7X_LANES):
                out_ref[b, :, s * V7X_LANES:(s + 1) * V7X_LANES] = (
                    res_ref[s, pl.ds(b, tpb, stride=nbatch), :])
    else:
        out_ref[...] = res


def _ffn_layer(h, n_rows, d, shift, tiles_per_seq, tm, layer, ffn_norm, wup, conv_w,
               conv_b, wdown, attn=None, batch_major_out=False, next_f32=None):
    f = wdown.shape[0]
    fc = V7X_MXU_DIM
    n_fc = f // fc
    n_steps = n_rows // tm
    halo = -(-2 * shift // V7X_SUBLANES) * V7X_SUBLANES
    aligned = shift % V7X_SUBLANES == 0
    n_slots = 4
    row_map = lambda i: (i, 0)
    in_specs = [pl.BlockSpec((tm, d), row_map)]
    args = [h]
    if attn is not None:
        o, wo, j = attn
        in_specs += [pl.BlockSpec((tm, d), row_map), _layer_resident(wo, j)]
        args += [o, wo]
    in_specs += [_layer_resident(ffn_norm, layer), _resident(wup.shape),
                 _layer_resident(conv_w, layer), _layer_resident(conv_b, layer),
                 _resident(wdown.shape)]
    args += [ffn_norm, wup, conv_w, conv_b, wdown]
    scratch = [
        pltpu.VMEM((2 * n_fc, halo, fc), F32),
        pltpu.VMEM((tm, d), BF16),
        pltpu.VMEM((tm, f), BF16),
    ]
    if not aligned:
        scratch.append(pltpu.VMEM((n_slots, tm + halo, fc), F32))
    if batch_major_out:
        scratch += [pltpu.VMEM((tm, d), F32), pltpu.VMEM((d // V7X_LANES, tm, V7X_LANES), F32)]
        out_specs = [pl.BlockSpec((shift, tm // shift, d), lambda i: (0, i, 0))]
        out_shape = [jax.ShapeDtypeStruct((shift, n_rows // shift, d), F32)]
    else:
        out_specs = [pl.BlockSpec((tm, d), row_map)]
        out_shape = [jax.ShapeDtypeStruct((n_rows, d), F32)]
    if next_f32 is not None:
        for w in next_f32:
            rb = _cast_rows(w.shape[1], n_steps)
            last = w.shape[1] // rb - 1
            in_specs.append(pl.BlockSpec(
                (None, rb, w.shape[2]),
                functools.partial(lambda i, last: (layer + 1, jnp.minimum(i, last), 0), last=last)))
            args.append(w)
            out_specs.append(pl.BlockSpec(
                (rb, w.shape[2]),
                functools.partial(lambda i, last: (jnp.minimum(i, last), 0), last=last)))
            out_shape.append(jax.ShapeDtypeStruct(w.shape[1:], BF16))
    est = (d * 2 * f * 2 + f * d * 2 + d * d * 2 + 8 * tm * d * 4 + tm * d * 2 + tm * f * 2
           + n_slots * (tm + halo) * fc * 4 + 2 * n_fc * halo * fc * 4 + 8 * tm * fc * 4)
    kern = functools.partial(_ffn_kernel, has_attn=attn is not None, shift=shift, halo=halo,
                             tiles_per_seq=tiles_per_seq, n_fc=n_fc, fc=fc, tm=tm,
                             n_slots=n_slots, batch_major_out=batch_major_out,
                             cast_next=next_f32 is not None)
    outs = pl.pallas_call(
        kern,
        grid=(n_steps,),
        in_specs=in_specs,
        out_specs=out_specs,
        out_shape=out_shape,
        scratch_shapes=scratch,
        compiler_params=_params(("arbitrary",), est),
        name="conv_ffn_attn" if attn is not None else "conv_ffn",
    )(*args)
    return outs if next_f32 is not None else outs[0]


def _rope(x, cc, ss, dh):
    return x * cc + pltpu.roll(x, dh // 2, axis=1) * ss


def _kv_kernel(h_ref, nrm_ref, wk_ref, wvt_ref, knorm_ref, cc_ref, ss_ref,
               kaug_ref, vt_ref, kmean_ref, *, n_heads, dh):
    xn = _rmsnorm(h_ref[...], nrm_ref[...]).astype(BF16)
    kv = jnp.dot(xn, wk_ref[...], preferred_element_type=F32)
    vt = lax.dot_general(wvt_ref[...], xn, (((1,), (1,)), ((), ())),
                         preferred_element_type=F32)
    rows = kv.shape[0]
    blk = pl.program_id(1)
    lane = lax.broadcasted_iota(jnp.int32, (rows, dh), 1)
    onehot = jnp.where(lane == blk, 1.0, 0.0).astype(BF16)
    ones_row = jnp.where(lax.broadcasted_iota(jnp.int32, (V_EXTRA_ROWS, rows), 0) == 0,
                         1.0, 0.0).astype(BF16)
    cc = cc_ref[...]
    ss = ss_ref[...]
    means = []
    for hd in range(n_heads):
        kh = _rmsnorm(kv[:, hd * dh:(hd + 1) * dh], knorm_ref[...])
        kr = _rope(kh, cc, ss, dh)
        kaug_ref[0, hd, :, pl.ds(0, dh)] = kr.astype(BF16)
        kaug_ref[0, hd, :, pl.ds(dh, dh)] = onehot
        vt_ref[0, hd, pl.ds(0, dh), :] = vt[hd * dh:(hd + 1) * dh, :].astype(BF16)
        vt_ref[0, hd, pl.ds(dh, V_EXTRA_ROWS), :] = ones_row
        means.append(jnp.mean(kr, axis=0, keepdims=True))
    kmean_ref[0, 0] = jnp.concatenate(means, axis=0)


def _kv_proj(h, nbatch, length, d, kv_norm, w_kv, k_norm, cc, ss):
    dh = k_norm.shape[0]
    n_heads = w_kv.shape[1] // (2 * dh)
    nblk = length // MOBA_BLOCK
    t = MOBA_BLOCK
    est = w_kv.size * 2 + 2 * t * d * 4 + 3 * t * 2 * n_heads * dh * 4 + 6 * n_heads * t * dh * 2
    return pl.pallas_call(
        functools.partial(_kv_kernel, n_heads=n_heads, dh=dh),
        grid=(nbatch, nblk),
        in_specs=[
            pl.BlockSpec((t, d), lambda b, i: (b * nblk + i, 0)),
            _resident((1, d)),
            _resident((d, n_heads * dh)),
            _resident((n_heads * dh, d)),
            _resident((1, dh)),
            pl.BlockSpec((t, dh), lambda b, i: (i, 0)),
            pl.BlockSpec((t, dh), lambda b, i: (i, 0)),
        ],
        out_specs=[
            pl.BlockSpec((1, n_heads, t, 2 * dh), lambda b, i: (b, 0, i, 0)),
            pl.BlockSpec((1, n_heads, dh + V_EXTRA_ROWS, t), lambda b, i: (b, 0, 0, i)),
            pl.BlockSpec((1, 1, n_heads, dh), lambda b, i: (b, i, 0, 0)),
        ],
        out_shape=[
            jax.ShapeDtypeStruct((nbatch, n_heads, length, 2 * dh), BF16),
            jax.ShapeDtypeStruct((nbatch, n_heads, dh + V_EXTRA_ROWS, length), BF16),
            jax.ShapeDtypeStruct((nbatch, nblk, n_heads, dh), F32),
        ],
        compiler_params=_params(("parallel", "parallel"), est),
        name="kv_proj",
    )(h, kv_norm.reshape(1, d), w_kv[:, :n_heads * dh].astype(BF16),
      w_kv[:, n_heads * dh:].T.astype(BF16), k_norm.reshape(1, dh), cc, ss)


def _q_kernel(h_ref, nrm_ref, wq_ref, qnorm_ref, cc_ref, ss_ref, kmean_ref, qaug_ref,
              *, n_heads, dh, nblk, topk, scale):
    xn = _rmsnorm(h_ref[...], nrm_ref[...]).astype(BF16)
    q = jnp.dot(xn, wq_ref[...], preferred_element_type=F32)
    rows = q.shape[0]
    cur = pl.program_id(1)
    cc = cc_ref[...]
    ss = ss_ref[...]
    blk_id = lax.broadcasted_iota(jnp.int32, (nblk, rows), 0)
    for hd in range(n_heads):
        qh = _rmsnorm(q[:, hd * dh:(hd + 1) * dh], qnorm_ref[...])
        qr = _rope(qh, cc, ss, dh)
        gate_t = lax.dot_general(kmean_ref[0, hd], qr, (((1,), (1,)), ((), ())),
                                 precision=lax.Precision.HIGHEST,
                                 preferred_element_type=F32)
        gate_t = jnp.where(blk_id < cur, gate_t, -jnp.inf)
        sel_rows = []
        for j in range(nblk):
            gj = gate_t[j:j + 1, :]
            beats = jnp.where((gate_t > gj) | ((gate_t == gj) & (blk_id < j)), 1.0, 0.0)
            rank = jnp.sum(beats, axis=0, keepdims=True)
            sel_rows.append(
                jnp.where(((rank < topk) & (j < cur)) | (j == cur), 0.0, MASK_BIAS))
        bias_t = jnp.concatenate(
            sel_rows + [jnp.zeros((dh - nblk, rows), F32)], axis=0)
        qaug_ref[0, hd, :, pl.ds(0, dh)] = (qr * scale).astype(BF16)
        qaug_ref[0, hd, :, pl.ds(dh, dh)] = bias_t.T.astype(BF16)


def _q_proj(h, nbatch, length, d, b_norm, w_q, q_norm, cc, ss, kmean):
    dh = q_norm.shape[0]
    n_heads = w_q.shape[1] // dh
    nblk = length // MOBA_BLOCK
    t = MOBA_BLOCK
    est = w_q.size * 2 + 2 * t * d * 4 + 3 * t * n_heads * dh * 4 + 4 * n_heads * t * dh * 2
    kern = functools.partial(_q_kernel, n_heads=n_heads, dh=dh, nblk=nblk,
                             topk=min(MOBA_TOPK, nblk), scale=dh ** -0.5 * LOG2_E)
    return pl.pallas_call(
        kern,
        grid=(nbatch, nblk),
        in_specs=[
            pl.BlockSpec((t, d), lambda b, i: (b * nblk + i, 0)),
            _resident((1, d)),
            _resident(w_q.shape),
            _resident((1, dh)),
            pl.BlockSpec((t, dh), lambda b, i: (i, 0)),
            pl.BlockSpec((t, dh), lambda b, i: (i, 0)),
            pl.BlockSpec((1, n_heads, nblk, dh), lambda b, i: (b, 0, 0, 0)),
        ],
        out_specs=pl.BlockSpec((1, n_heads, t, 2 * dh), lambda b, i: (b, 0, i, 0)),
        out_shape=jax.ShapeDtypeStruct((nbatch, n_heads, length, 2 * dh), BF16),
        compiler_params=_params(("parallel", "parallel"), est),
        name="q_proj_gate",
    )(h, b_norm.reshape(1, d), w_q.astype(BF16), q_norm.reshape(1, dh), cc, ss, kmean)


def _attn_kernel(q_ref, k_ref, vt_ref, o_ref, s_ref, *, blk, nblk, dh):
    nt = (((1,), (1,)), ((), ()))
    kpos = lax.broadcasted_iota(jnp.int32, (blk, blk), 0)
    qpos = lax.broadcasted_iota(jnp.int32, (blk, blk), 1)
    causal = kpos <= qpos

    def score_step(c, j, m_acc):
        st = lax.dot_general(k_ref[0, 0, j * blk:(j + 1) * blk, :],
                             q_ref[0, 0, c * blk:(c + 1) * blk, :], nt,
                             preferred_element_type=F32)
        if j == c:
            st = jnp.where(causal, st, -jnp.inf)
        s_ref[c % 4, j * blk:(j + 1) * blk, :] = st
        part = jnp.max(st.reshape(blk // V7X_SUBLANES, V7X_SUBLANES, blk), axis=0)
        return part if m_acc is None else jnp.maximum(m_acc, part)

    def value_step(c, j, m, acc):
        pt = jnp.exp2(s_ref[c % 4, j * blk:(j + 1) * blk, :] - m)
        pv = jnp.dot(vt_ref[0, 0, :, j * blk:(j + 1) * blk], pt.astype(BF16),
                     preferred_element_type=F32)
        return pv if acc is None else acc + pv

    def reduce_max(m_acc):
        return jnp.max(m_acc, axis=0, keepdims=True)

    ahead = min(3, nblk - 1)
    m_row = {}
    for c in range(ahead):
        m_acc = None
        for j in range(c + 1):
            m_acc = score_step(c, j, m_acc)
        m_row[c] = reduce_max(m_acc)
    for c in range(nblk):
        acc = m_next = None
        nxt = c + ahead
        for j in range(max(c + 1, nxt + 1 if nxt < nblk else 0)):
            if nxt < nblk and j <= nxt:
                m_next = score_step(nxt, j, m_next)
            if j <= c:
                acc = value_step(c, j, m_row[c], acc)
        o_ref[c * blk:(c + 1) * blk, :] = (acc[:dh] / acc[dh:dh + 1]).T.astype(o_ref.dtype)
        if nxt < nblk:
            m_row[nxt] = reduce_max(m_next)


def _attention(qaug, kaug, vt):
    nbatch, n_heads, length, dh2 = qaug.shape
    dh = dh2 // 2
    blk = MOBA_BLOCK
    nblk = length // blk
    est = 4 * length * dh2 * 2 + 4 * length * dh * 2 + 4 * blk * length * 4
    return pl.pallas_call(
        functools.partial(_attn_kernel, blk=blk, nblk=nblk, dh=dh),
        grid=(nbatch, n_heads),
        in_specs=[
            pl.BlockSpec((1, 1, length, dh2), lambda b, h: (b, h, 0, 0)),
            pl.BlockSpec((1, 1, length, dh2), lambda b, h: (b, h, 0, 0)),
            pl.BlockSpec((1, 1, vt.shape[2], length), lambda b, h: (b, h, 0, 0)),
        ],
        out_specs=pl.BlockSpec((length, dh), lambda b, h: (b, h)),
        out_shape=jax.ShapeDtypeStruct((nbatch * length, n_heads * dh), BF16),
        scratch_shapes=[pltpu.VMEM((4, length, blk), F32)],
        compiler_params=_params(("parallel", "parallel"), est),
        name="moba_attention",
    )(qaug, kaug, vt)


def _rope_tables(length, dh):
    half = dh // 2
    inv_freq = ROPE_THETA ** (-jnp.arange(half, dtype=F32) * 2.0 / dh)
    ang = jnp.arange(length, dtype=F32)[:, None] * inv_freq[None, :]
    cos, sin = jnp.cos(ang), jnp.sin(ang)
    return jnp.concatenate([cos, cos], axis=-1), jnp.concatenate([-sin, sin], axis=-1)


def kernel(x, a_norm, s5_A_re, s5_A_im, s5_log_dt, s5_B_re, s5_B_im, s5_C_re, s5_C_im, s5_D,
           w_glu, b_glu, kv_norm, w_kv, k_norm, b_norm, w_q, q_norm, w_o, ffn_norm, w_up,
           conv_w, conv_b, w_down):
    nbatch, length, d = x.shape
    n_a = a_norm.shape[0]
    n_b = b_norm.shape[0]
    rows = nbatch * length
    assert nbatch == V7X_SUBLANES, "the S5 recurrence keeps the batch on the sublane axis"
    assert length % MOBA_BLOCK == 0 and n_a >= 1 and n_b >= 1
    dh = k_norm.shape[0]
    cc, ss = _rope_tables(length, dh)

    depth = n_a + n_b
    g = s5_A_re.shape[1]
    w2, w3, w1, lam2 = _s5_prepare(s5_A_re, s5_A_im, s5_log_dt, s5_B_re, s5_B_im, s5_C_re,
                                   s5_C_im, min(g, V7X_LANES // (d // g)))
    a_norm3, d_skip3 = a_norm.reshape(n_a, 1, d), s5_D.reshape(n_a, 1, d)
    wglu, b_glu3 = w_glu.astype(BF16), b_glu.reshape(n_a, 1, 2 * d)
    fnorm3 = ffn_norm.reshape(depth, 1, d)
    conv_b3 = conv_b.reshape(depth, 1, conv_b.shape[-1])
    wo = w_o.astype(BF16)

    h = x
    tm_a = min(1024, rows)
    for i in range(n_a):
        res = _s5_layer(h, nbatch, length, d, i, i == 0, a_norm3, w2, lam2, w3, w1, d_skip3,
                        wglu, b_glu3, next_f32=(w_up, w_down) if i == 0 else None)
        h, wup, wdown = res if i == 0 else (res, wup, wdown)
        h, wup, wdown = _ffn_layer(h.reshape(rows, d), rows, d, nbatch, rows // tm_a, tm_a, i,
                                   fnorm3, wup, conv_w, conv_b3, wdown,
                                   batch_major_out=(i == n_a - 1), next_f32=(w_up, w_down))
        if i < n_a - 1:
            h = h.reshape(length // 2, 2, nbatch, d)

    h = h.reshape(rows, d)
    kaug, vt, kmean = _kv_proj(h, nbatch, length, d, kv_norm, w_kv, k_norm, cc, ss)
    kmean = kmean.transpose(0, 2, 1, 3)
    tm_b = min(512, length)
    for j in range(n_b):
        qaug = _q_proj(h, nbatch, length, d, b_norm[j], w_q[j], q_norm[j], cc, ss, kmean)
        o = _attention(qaug, kaug, vt)
        last = j == n_b - 1
        res = _ffn_layer(h, rows, d, 1, length // tm_b, tm_b, n_a + j, fnorm3, wup, conv_w,
                         conv_b3, wdown, attn=(o, wo, j),
                         next_f32=None if last else (w_up, w_down))
        h, wup, wdown = (res, None, None) if last else res
    return h.reshape(nbatch, length, d)
```

```python
import functools

import jax
import jax.numpy as jnp
from jax import lax
from jax.experimental import pallas as pl
from jax.experimental.pallas import tpu as pltpu

MOBA_BLOCK = 256
MOBA_TOPK = 3
CONV_WIDTH = 3
ROPE_THETA = 10000.0
EPS = 1e-6

V7X_SUBLANES = 8
V7X_LANES = 128
V7X_MXU_DIM = 256
V7X_VMEM_BYTES = 64 * 1024 * 1024

MASK_BIAS = -1e30
LOG2_E = 1.4426950408889634
V_EXTRA_ROWS = 16

F32 = jnp.float32
BF16 = jnp.bfloat16


def _vmem_limit(est_bytes):
    return int(min(V7X_VMEM_BYTES - 6 * 1024 * 1024, max(32 * 1024 * 1024, 2 * est_bytes)))


def _params(semantics, est_bytes):
    return pltpu.CompilerParams(dimension_semantics=semantics,
                                vmem_limit_bytes=_vmem_limit(est_bytes))


def _resident(shape):
    nd = len(shape)
    return pl.BlockSpec(shape, lambda *_: (0,) * nd, pipeline_mode=pl.Buffered(1))


def _layer_resident(arr, layer):
    nd = arr.ndim - 1
    return pl.BlockSpec((None,) + arr.shape[1:], lambda *_: (layer,) + (0,) * nd,
                        pipeline_mode=pl.Buffered(1))


def _cast_rows(total, n_steps):
    tile = 2 * V7X_SUBLANES
    return next(r for r in range(tile, total + 1, tile) if total % r == 0 and total // r <= n_steps)


def _rmsnorm(x, g):
    return x * lax.rsqrt(jnp.mean(x * x, axis=-1, keepdims=True) + EPS) * g


def _s5_kernel(*refs, n_slabs, slab_ch, nstate, nbatch, npairs, batch_major_in, cast_next):
    n_in = n_slabs * slab_ch // V7X_LANES if batch_major_in else 1
    h_refs = refs[:n_in]
    (anorm_ref, w2_ref, lam2_ref, w3_ref, w1_ref, dskip_ref, wglu_ref, bglu_ref,
     *refs) = refs[n_in:]
    if cast_next:
        wup_f32_ref, wdown_f32_ref, out_ref, wup_next_ref, wdown_next_ref, *refs = refs
        wup_next_ref[...] = wup_f32_ref[...].astype(BF16)
        wdown_next_ref[...] = wdown_f32_ref[...].astype(BF16)
    else:
        out_ref, *refs = refs
    state_ref, z_ref, x_ref, xn_ref, pair_ref, g_ref, *xt = refs
    half = npairs * nbatch
    d = x_ref.shape[-1]

    @pl.when(pl.program_id(0) == 0)
    def _():
        state_ref[...] = jnp.zeros_like(state_ref)

    if batch_major_in:
        xt_ref = xt[0]
        for s in range(n_in):
            for b in range(nbatch):
                for par in range(2):
                    xt_ref[s, pl.ds(par * half + b, npairs, stride=nbatch), :] = (
                        h_refs[s][b, pl.ds(par, npairs, stride=2), :])
            x_ref[:, s * V7X_LANES:(s + 1) * V7X_LANES] = xt_ref[s]
    else:
        h_ref = h_refs[0]
        x_ref[pl.ds(0, half), :] = h_ref[:, 0].reshape(half, d)
        x_ref[pl.ds(half, half), :] = h_ref[:, 1].reshape(half, d)

    xn = _rmsnorm(x_ref[...], anorm_ref[...])
    xn_ref[...] = xn
    xnb = xn.astype(BF16)

    for s in range(n_slabs):
        cs = slice(s * slab_ch, (s + 1) * slab_ch)
        pair_ref[s] = jnp.concatenate([xnb[:half, cs], xnb[half:, cs]], axis=1)
        zb = z_ref.at[s % 2]
        zb[...] = jnp.dot(pair_ref[s], w2_ref[s], preferred_element_type=F32)
        lr = jnp.broadcast_to(lam2_ref[s, 0:1, :], (nbatch, nstate))
        li = jnp.broadcast_to(lam2_ref[s, 1:2, :], (nbatch, nstate))
        sr = state_ref[s, :, pl.ds(0, nstate)]
        si = state_ref[s, :, pl.ds(nstate, nstate)]
        for k in range(npairs):
            rows = pl.ds(k * nbatch, nbatch)
            zr = zb[rows, pl.ds(0, nstate)]
            zi = zb[rows, pl.ds(nstate, nstate)]
            zb[rows, pl.ds(0, nstate)] = sr
            zb[rows, pl.ds(nstate, nstate)] = si
            sr, si = lr * sr - li * si + zr, lr * si + li * sr + zi
        state_ref[s, :, pl.ds(0, nstate)] = sr
        state_ref[s, :, pl.ds(nstate, nstate)] = si

        y = (jnp.dot(zb[...].astype(BF16), w3_ref[s], preferred_element_type=F32)
             + jnp.dot(pair_ref[s], w1_ref[s], preferred_element_type=F32))
        dsk = dskip_ref[:, cs]
        g_ref[pl.ds(0, half), cs] = jax.nn.gelu(
            y[:, :slab_ch] + dsk * xn_ref[pl.ds(0, half), cs]).astype(BF16)
        g_ref[pl.ds(half, half), cs] = jax.nn.gelu(
            y[:, slab_ch:] + dsk * xn_ref[pl.ds(half, half), cs]).astype(BF16)

    nc = V7X_MXU_DIM
    for c0 in range(0, d, nc):
        ca, cb = slice(c0, c0 + nc), slice(d + c0, d + c0 + nc)
        za = jnp.dot(g_ref[...], wglu_ref[:, ca], preferred_element_type=F32) + bglu_ref[:, ca]
        zb = jnp.dot(g_ref[...], wglu_ref[:, cb], preferred_element_type=F32) + bglu_ref[:, cb]
        res = x_ref[:, ca] + za * jax.nn.sigmoid(zb)
        out_ref[:, 0, :, ca] = res[:half].reshape(npairs, nbatch, nc)
        out_ref[:, 1, :, ca] = res[half:].reshape(npairs, nbatch, nc)


def _block_diag_lanes(src, n_blocks, row_block, col_block):
    rows, k = src.shape[-3], src.shape[-2]
    kin, kout = k * col_block, k * n_blocks * col_block
    ci = lax.broadcasted_iota(jnp.int32, (kin, kout), 0)
    co = lax.broadcasted_iota(jnp.int32, (kin, kout), 1)
    expand = jnp.where((ci // col_block == co // (n_blocks * col_block))
                       & (ci % col_block == co % col_block), 1.0, 0.0)
    wide = jnp.einsum('...rk,kn->...rn', src.reshape(src.shape[:-2] + (kin,)), expand,
                      precision=lax.Precision.HIGHEST)
    g = (lax.broadcasted_iota(jnp.int32, (rows, kout), 0) // row_block) % n_blocks
    h = (lax.broadcasted_iota(jnp.int32, (rows, kout), 1) // col_block) % n_blocks
    return jnp.where(g == h, wide, 0.0)


def _s5_prepare(a_re, a_im, log_dt, b_re, b_im, c_re, c_im, gs):
    na, g, p = a_re.shape
    c = b_re.shape[-1]
    ns = g // gs
    hi = lax.Precision.HIGHEST
    lam = lax.complex(a_re, a_im)
    dt = jnp.exp(log_dt)[..., None]
    lam_bar = jnp.exp(lam * dt)
    b_bar = ((lam_bar - 1.0) / lam)[..., None] * lax.complex(b_re, b_im)
    c_mat = lax.complex(c_re, c_im)
    lb = lam_bar[..., None] * b_bar
    cl1 = c_mat * lam_bar[:, :, None, :]
    cl2 = cl1 * lam_bar[:, :, None, :]
    lam2 = lam_bar * lam_bar

    def re_prod(cm, bm):
        return (jnp.einsum('lgdp,lgpc->lgdc', jnp.real(cm), jnp.real(bm), precision=hi)
                - jnp.einsum('lgdp,lgpc->lgdc', jnp.imag(cm), jnp.imag(bm), precision=hi))

    rcb = re_prod(c_mat, b_bar)
    rclb = re_prod(cl1, b_bar)

    s2 = jnp.stack([lb, b_bar], axis=1).reshape(na, 2, ns, gs, p, c)
    s2 = s2.transpose(0, 2, 1, 3, 5, 4)
    s2 = jnp.stack([jnp.real(s2), jnp.imag(s2)], axis=-2)
    w2 = _block_diag_lanes(s2.reshape(na, ns, 2 * gs * c, 2, p), gs, c, p)
    s3 = jnp.stack([cl1, cl2], axis=1).reshape(na, 2, ns, gs, c, p)
    s3 = s3.transpose(0, 2, 3, 5, 1, 4)
    s3 = jnp.stack([jnp.real(s3), -jnp.imag(s3)], axis=2)
    w3 = _block_diag_lanes(s3.reshape(na, ns, 2 * gs * p, 2, c), gs, p, c)
    zero = jnp.zeros_like(rcb)
    s1 = jnp.stack([jnp.stack([rcb, rclb], axis=1), jnp.stack([zero, rcb], axis=1)], axis=1)
    s1 = s1.reshape(na, 2, 2, ns, gs, c, c)
    s1 = s1.transpose(0, 3, 1, 4, 6, 2, 5)
    w1 = _block_diag_lanes(s1.reshape(na, ns, 2 * gs * c, 2, c), gs, c, c)
    lam2_t = jnp.stack([jnp.real(lam2).reshape(na, ns, gs * p),
                        jnp.imag(lam2).reshape(na, ns, gs * p)], axis=2)
    return w2.astype(BF16), w3.astype(BF16), w1.astype(BF16), lam2_t


def _s5_layer(h, nbatch, length, d, layer, batch_major_in, a_norm, w2, lam2, w3, w1, d_skip,
              wglu, b_glu, next_f32=None):
    n_slabs, slab_ch = w2.shape[1], w2.shape[2] // 2
    nstate = w2.shape[3] // 2
    npairs = min(32, length // 2)
    half = npairs * nbatch
    tb = 2 * half
    if batch_major_in:
        n_in = d // V7X_LANES
        h_specs = [pl.BlockSpec((nbatch, 2 * npairs, V7X_LANES), functools.partial(
            lambda i, s: (0, i, s), s=s)) for s in range(n_in)]
        h_args = [h] * n_in
        extra = [pltpu.VMEM((n_in, tb, V7X_LANES), F32)]
    else:
        h_specs = [pl.BlockSpec((npairs, 2, nbatch, d), lambda i: (i, 0, 0, 0))]
        h_args = [h]
        extra = []
    est = (6 * tb * d * 4 + 2 * half * 2 * nstate * 4 + tb * d * 8 + 3 * tb * d * 2
           + n_slabs * slab_ch * 2 * (4 * nstate + 2 * slab_ch) * 2 + 2 * d * d * 2
           + 2 * tb * 2 * d * 4)
    kern = functools.partial(_s5_kernel, n_slabs=n_slabs, slab_ch=slab_ch, nstate=nstate,
                             nbatch=nbatch, npairs=npairs, batch_major_in=batch_major_in,
                             cast_next=next_f32 is not None)
    n_steps = length // (2 * npairs)
    cast_specs, cast_args = [], []
    out_specs = [pl.BlockSpec((npairs, 2, nbatch, d), lambda i: (i, 0, 0, 0))]
    out_shape = [jax.ShapeDtypeStruct((length // 2, 2, nbatch, d), F32)]
    for w in next_f32 or ():
        rb = _cast_rows(w.shape[1], n_steps)
        last = w.shape[1] // rb - 1
        cast_specs.append(pl.BlockSpec(
            (None, rb, w.shape[2]),
            functools.partial(lambda i, last: (layer, jnp.minimum(i, last), 0), last=last)))
        cast_args.append(w)
        out_specs.append(pl.BlockSpec(
            (rb, w.shape[2]),
            functools.partial(lambda i, last: (jnp.minimum(i, last), 0), last=last)))
        out_shape.append(jax.ShapeDtypeStruct(w.shape[1:], BF16))
    outs = pl.pallas_call(
        kern,
        grid=(n_steps,),
        in_specs=h_specs + [
            _layer_resident(a_norm, layer),
            _layer_resident(w2, layer),
            _layer_resident(lam2, layer),
            _layer_resident(w3, layer),
            _layer_resident(w1, layer),
            _layer_resident(d_skip, layer),
            _layer_resident(wglu, layer),
            _layer_resident(b_glu, layer),
        ] + cast_specs,
        out_specs=out_specs,
        out_shape=out_shape,
        scratch_shapes=[
            pltpu.VMEM((n_slabs, nbatch, 2 * nstate), F32),
            pltpu.VMEM((2, half, 2 * nstate), F32),
            pltpu.VMEM((tb, d), F32),
            pltpu.VMEM((tb, d), F32),
            pltpu.VMEM((n_slabs, half, 2 * slab_ch), BF16),
            pltpu.VMEM((tb, d), BF16),
        ] + extra,
        compiler_params=_params(("arbitrary",), est),
        name="s5_mixer",
    )(*h_args, a_norm, w2, lam2, w3, w1, d_skip, wglu, b_glu, *cast_args)
    return outs if next_f32 is not None else outs[0]


def _ffn_kernel(*refs, has_attn, shift, halo, tiles_per_seq, n_fc, fc, tm, n_slots,
                batch_major_out, cast_next):
    refs = list(refs)
    h_ref = refs.pop(0)
    o_ref, wo_ref = (refs.pop(0), refs.pop(0)) if has_attn else (None, None)
    fnorm_ref, wup_ref, cw_ref, cb_ref, wdown_ref = refs[:5]
    refs = refs[5:]
    if cast_next:
        wup_f32_ref, wdown_f32_ref, out_ref, wup_next_ref, wdown_next_ref = refs[:5]
        refs = refs[5:]
        wup_next_ref[...] = wup_f32_ref[...].astype(BF16)
        wdown_next_ref[...] = wdown_f32_ref[...].astype(BF16)
    else:
        out_ref = refs.pop(0)
    carry_ref, xnb_ref, act_ref, *hu = refs
    aligned = shift % V7X_SUBLANES == 0

    @pl.when(pl.program_id(0) % tiles_per_seq == 0)
    def _():
        carry_ref[...] = jnp.zeros_like(carry_ref)

    if batch_major_out:
        *hu, h1_ref, res_ref = hu
    else:
        h1_ref = out_ref
    h1 = h_ref[...]
    if has_attn:
        h1 = h1 + jnp.dot(o_ref[...], wo_ref[...], preferred_element_type=F32)
    h1_ref[...] = h1
    xnb_ref[...] = _rmsnorm(h1, fnorm_ref[...]).astype(BF16)

    def conv_branch(idx, slot):
        cols = slice(idx * fc, (idx + 1) * fc)
        u = jnp.dot(xnb_ref[...], wup_ref[:, cols], preferred_element_type=F32)
        w = cw_ref[:, cols]
        if aligned:
            prev = carry_ref[idx]
            carry_ref[idx] = u[tm - halo:, :]
            u1 = jnp.concatenate([prev[halo - shift:, :], u[:tm - shift, :]], axis=0)
            u2 = jnp.concatenate([prev[halo - 2 * shift:, :], u[:tm - 2 * shift, :]], axis=0)
        else:
            hu_ref = hu[0]
            hu_ref[slot, pl.ds(0, halo), :] = carry_ref[idx]
            hu_ref[slot, pl.ds(halo, tm), :] = u
            carry_ref[idx] = u[tm - halo:, :]
            u1 = hu_ref[slot, pl.ds(halo - shift, tm), :]
            u2 = hu_ref[slot, pl.ds(halo - 2 * shift, tm), :]
        return w[2:3, :] * u + w[1:2, :] * u1 + w[0:1, :] * u2 + cb_ref[:, cols]

    for c in range(n_fc):
        gate = conv_branch(c, (2 * c) % n_slots)
        val = conv_branch(n_fc + c, (2 * c + 1) % n_slots)
        act_ref[:, c * fc:(c + 1) * fc] = (jax.nn.silu(gate) * val).astype(BF16)

    res = h1_ref[...] + jnp.dot(act_ref[...], wdown_ref[...], preferred_element_type=F32)
    if batch_major_out:
        nbatch, tpb, d = out_ref.shape
        for s in range(d // V7X_LANES):
            res_ref[s] = res[:, s * V7X_LANES:(s + 1) * V7X_LANES]
        for b in range(nbatch):
            for s in range(d // V7X_LANES):
                out_ref[b, :, s * V7X_LANES:(s + 1) * V7X_LANES] = (
                    res_ref[s, pl.ds(b, tpb, stride=nbatch), :])
    else:
        out_ref[...] = res


def _ffn_layer(h, n_rows, d, shift, tiles_per_seq, tm, layer, ffn_norm, wup, conv_w,
               conv_b, wdown, attn=None, batch_major_out=False, next_f32=None):
    f = wdown.shape[0]
    fc = V7X_MXU_DIM
    n_fc = f // fc
    n_steps = n_rows // tm
    halo = -(-2 * shift // V7X_SUBLANES) * V7X_SUBLANES
    aligned = shift % V7X_SUBLANES == 0
    n_slots = 4
    row_map = lambda i: (i, 0)
    in_specs = [pl.BlockSpec((tm, d), row_map)]
    args = [h]
    if attn is not None:
        o, wo, j = attn
        in_specs += [pl.BlockSpec((tm, d), row_map), _layer_resident(wo, j)]
        args += [o, wo]
    in_specs += [_layer_resident(ffn_norm, layer), _resident(wup.shape),
                 _layer_resident(conv_w, layer), _layer_resident(conv_b, layer),
                 _resident(wdown.shape)]
    args += [ffn_norm, wup, conv_w, conv_b, wdown]
    scratch = [
        pltpu.VMEM((2 * n_fc, halo, fc), F32),
        pltpu.VMEM((tm, d), BF16),
        pltpu.VMEM((tm, f), BF16),
    ]
    if not aligned:
        scratch.append(pltpu.VMEM((n_slots, tm + halo, fc), F32))
    if batch_major_out:
        scratch += [pltpu.VMEM((tm, d), F32), pltpu.VMEM((d // V7X_LANES, tm, V7X_LANES), F32)]
        out_specs = [pl.BlockSpec((shift, tm // shift, d), lambda i: (0, i, 0))]
        out_shape = [jax.ShapeDtypeStruct((shift, n_rows // shift, d), F32)]
    else:
        out_specs = [pl.BlockSpec((tm, d), row_map)]
        out_shape = [jax.ShapeDtypeStruct((n_rows, d), F32)]
    if next_f32 is not None:
        for w in next_f32:
            rb = _cast_rows(w.shape[1], n_steps)
            last = w.shape[1] // rb - 1
            in_specs.append(pl.BlockSpec(
                (None, rb, w.shape[2]),
                functools.partial(lambda i, last: (layer + 1, jnp.minimum(i, last), 0), last=last)))
            args.append(w)
            out_specs.append(pl.BlockSpec(
                (rb, w.shape[2]),
                functools.partial(lambda i, last: (jnp.minimum(i, last), 0), last=last)))
            out_shape.append(jax.ShapeDtypeStruct(w.shape[1:], BF16))
    est = (d * 2 * f * 2 + f * d * 2 + d * d * 2 + 8 * tm * d * 4 + tm * d * 2 + tm * f * 2
           + n_slots * (tm + halo) * fc * 4 + 2 * n_fc * halo * fc * 4 + 8 * tm * fc * 4)
    kern = functools.partial(_ffn_kernel, has_attn=attn is not None, shift=shift, halo=halo,
                             tiles_per_seq=tiles_per_seq, n_fc=n_fc, fc=fc, tm=tm,
                             n_slots=n_slots, batch_major_out=batch_major_out,
                             cast_next=next_f32 is not None)
    outs = pl.pallas_call(
        kern,
        grid=(n_steps,),
        in_specs=in_specs,
        out_specs=out_specs,
        out_shape=out_shape,
        scratch_shapes=scratch,
        compiler_params=_params(("arbitrary",), est),
        name="conv_ffn_attn" if attn is not None else "conv_ffn",
    )(*args)
    return outs if next_f32 is not None else outs[0]


def _rope(x, cc, ss, dh):
    return x * cc + pltpu.roll(x, dh // 2, axis=1) * ss


def _kv_kernel(h_ref, nrm_ref, wk_ref, wvt_ref, knorm_ref, cc_ref, ss_ref,
               kaug_ref, vt_ref, kmean_ref, *, n_heads, dh):
    xn = _rmsnorm(h_ref[...], nrm_ref[...]).astype(BF16)
    kv = jnp.dot(xn, wk_ref[...], preferred_element_type=F32)
    vt = lax.dot_general(wvt_ref[...], xn, (((1,), (1,)), ((), ())),
                         preferred_element_type=F32)
    rows = kv.shape[0]
    blk = pl.program_id(1)
    lane = lax.broadcasted_iota(jnp.int32, (rows, dh), 1)
    onehot = jnp.where(lane == blk, 1.0, 0.0).astype(BF16)
    ones_row = jnp.where(lax.broadcasted_iota(jnp.int32, (V_EXTRA_ROWS, rows), 0) == 0,
                         1.0, 0.0).astype(BF16)
    cc = cc_ref[...]
    ss = ss_ref[...]
    means = []
    for hd in range(n_heads):
        kh = _rmsnorm(kv[:, hd * dh:(hd + 1) * dh], knorm_ref[...])
        kr = _rope(kh, cc, ss, dh)
        kaug_ref[0, hd, :, pl.ds(0, dh)] = kr.astype(BF16)
        kaug_ref[0, hd, :, pl.ds(dh, dh)] = onehot
        vt_ref[0, hd, pl.ds(0, dh), :] = vt[hd * dh:(hd + 1) * dh, :].astype(BF16)
        vt_ref[0, hd, pl.ds(dh, V_EXTRA_ROWS), :] = ones_row
        means.append(jnp.mean(kr, axis=0, keepdims=True))
    kmean_ref[0, 0] = jnp.concatenate(means, axis=0)


def _kv_proj(h, nbatch, length, d, kv_norm, w_kv, k_norm, cc, ss):
    dh = k_norm.shape[0]
    n_heads = w_kv.shape[1] // (2 * dh)
    nblk = length // MOBA_BLOCK
    t = MOBA_BLOCK
    est = w_kv.size * 2 + 2 * t * d * 4 + 3 * t * 2 * n_heads * dh * 4 + 6 * n_heads * t * dh * 2
    return pl.pallas_call(
        functools.partial(_kv_kernel, n_heads=n_heads, dh=dh),
        grid=(nbatch, nblk),
        in_specs=[
            pl.BlockSpec((t, d), lambda b, i: (b * nblk + i, 0)),
            _resident((1, d)),
            _resident((d, n_heads * dh)),
            _resident((n_heads * dh, d)),
            _resident((1, dh)),
            pl.BlockSpec((t, dh), lambda b, i: (i, 0)),
            pl.BlockSpec((t, dh), lambda b, i: (i, 0)),
        ],
        out_specs=[
            pl.BlockSpec((1, n_heads, t, 2 * dh), lambda b, i: (b, 0, i, 0)),
            pl.BlockSpec((1, n_heads, dh + V_EXTRA_ROWS, t), lambda b, i: (b, 0, 0, i)),
            pl.BlockSpec((1, 1, n_heads, dh), lambda b, i: (b, i, 0, 0)),
        ],
        out_shape=[
            jax.ShapeDtypeStruct((nbatch, n_heads, length, 2 * dh), BF16),
            jax.ShapeDtypeStruct((nbatch, n_heads, dh + V_EXTRA_ROWS, length), BF16),
            jax.ShapeDtypeStruct((nbatch, nblk, n_heads, dh), F32),
        ],
        compiler_params=_params(("parallel", "parallel"), est),
        name="kv_proj",
    )(h, kv_norm.reshape(1, d), w_kv[:, :n_heads * dh].astype(BF16),
      w_kv[:, n_heads * dh:].T.astype(BF16), k_norm.reshape(1, dh), cc, ss)


def _q_kernel(h_ref, nrm_ref, wq_ref, qnorm_ref, cc_ref, ss_ref, kmean_ref, qaug_ref,
              *, n_heads, dh, nblk, topk, scale):
    xn = _rmsnorm(h_ref[...], nrm_ref[...]).astype(BF16)
    q = jnp.dot(xn, wq_ref[...], preferred_element_type=F32)
    rows = q.shape[0]
    cur = pl.program_id(1)
    cc = cc_ref[...]
    ss = ss_ref[...]
    blk_id = lax.broadcasted_iota(jnp.int32, (nblk, rows), 0)
    for hd in range(n_heads):
        qh = _rmsnorm(q[:, hd * dh:(hd + 1) * dh], qnorm_ref[...])
        qr = _rope(qh, cc, ss, dh)
        gate_t = lax.dot_general(kmean_ref[0, hd], qr, (((1,), (1,)), ((), ())),
                                 precision=lax.Precision.HIGHEST,
                                 preferred_element_type=F32)
        gate_t = jnp.where(blk_id < cur, gate_t, -jnp.inf)
        sel_rows = []
        for j in range(nblk):
            gj = gate_t[j:j + 1, :]
            beats = jnp.where((gate_t > gj) | ((gate_t == gj) & (blk_id < j)), 1.0, 0.0)
            rank = jnp.sum(beats, axis=0, keepdims=True)
            sel_rows.append(
                jnp.where(((rank < topk) & (j < cur)) | (j == cur), 0.0, MASK_BIAS))
        bias_t = jnp.concatenate(
            sel_rows + [jnp.zeros((dh - nblk, rows), F32)], axis=0)
        qaug_ref[0, hd, :, pl.ds(0, dh)] = (qr * scale).astype(BF16)
        qaug_ref[0, hd, :, pl.ds(dh, dh)] = bias_t.T.astype(BF16)


def _q_proj(h, nbatch, length, d, b_norm, w_q, q_norm, cc, ss, kmean):
    dh = q_norm.shape[0]
    n_heads = w_q.shape[1] // dh
    nblk = length // MOBA_BLOCK
    t = MOBA_BLOCK
    est = w_q.size * 2 + 2 * t * d * 4 + 3 * t * n_heads * dh * 4 + 4 * n_heads * t * dh * 2
    kern = functools.partial(_q_kernel, n_heads=n_heads, dh=dh, nblk=nblk,
                             topk=min(MOBA_TOPK, nblk), scale=dh ** -0.5 * LOG2_E)
    return pl.pallas_call(
        kern,
        grid=(nbatch, nblk),
        in_specs=[
            pl.BlockSpec((t, d), lambda b, i: (b * nblk + i, 0)),
            _resident((1, d)),
            _resident(w_q.shape),
            _resident((1, dh)),
            pl.BlockSpec((t, dh), lambda b, i: (i, 0)),
            pl.BlockSpec((t, dh), lambda b, i: (i, 0)),
            pl.BlockSpec((1, n_heads, nblk, dh), lambda b, i: (b, 0, 0, 0)),
        ],
        out_specs=pl.BlockSpec((1, n_heads, t, 2 * dh), lambda b, i: (b, 0, i, 0)),
        out_shape=jax.ShapeDtypeStruct((nbatch, n_heads, length, 2 * dh), BF16),
        compiler_params=_params(("parallel", "parallel"), est),
        name="q_proj_gate",
    )(h, b_norm.reshape(1, d), w_q.astype(BF16), q_norm.reshape(1, dh), cc, ss, kmean)


def _attn_kernel(q_ref, k_ref, vt_ref, o_ref, s_ref, *, blk, nblk, dh):
    nt = (((1,), (1,)), ((), ()))
    kpos = lax.broadcasted_iota(jnp.int32, (blk, blk), 0)
    qpos = lax.broadcasted_iota(jnp.int32, (blk, blk), 1)
    causal = kpos <= qpos

    def score_step(c, j, m_acc):
        st = lax.dot_general(k_ref[0, 0, j * blk:(j + 1) * blk, :],
                             q_ref[0, 0, c * blk:(c + 1) * blk, :], nt,
                             preferred_element_type=F32)
        if j == c:
            st = jnp.where(causal, st, -jnp.inf)
        s_ref[c % 4, j * blk:(j + 1) * blk, :] = st
        part = jnp.max(st.reshape(blk // V7X_SUBLANES, V7X_SUBLANES, blk), axis=0)
        return part if m_acc is None else jnp.maximum(m_acc, part)

    def value_step(c, j, m, acc):
        pt = jnp.exp2(s_ref[c % 4, j * blk:(j + 1) * blk, :] - m)
        pv = jnp.dot(vt_ref[0, 0, :, j * blk:(j + 1) * blk], pt.astype(BF16),
                     preferred_element_type=F32)
        return pv if acc is None else acc + pv

    def reduce_max(m_acc):
        return jnp.max(m_acc, axis=0, keepdims=True)

    ahead = min(3, nblk - 1)
    m_row = {}
    for c in range(ahead):
        m_acc = None
        for j in range(c + 1):
            m_acc = score_step(c, j, m_acc)
        m_row[c] = reduce_max(m_acc)
    for c in range(nblk):
        acc = m_next = None
        nxt = c + ahead
        for j in range(max(c + 1, nxt + 1 if nxt < nblk else 0)):
            if nxt < nblk and j <= nxt:
                m_next = score_step(nxt, j, m_next)
            if j <= c:
                acc = value_step(c, j, m_row[c], acc)
        o_ref[c * blk:(c + 1) * blk, :] = (acc[:dh] / acc[dh:dh + 1]).T.astype(o_ref.dtype)
        if nxt < nblk:
            m_row[nxt] = reduce_max(m_next)


def _attention(qaug, kaug, vt):
    nbatch, n_heads, length, dh2 = qaug.shape
    dh = dh2 // 2
    blk = MOBA_BLOCK
    nblk = length // blk
    est = 4 * length * dh2 * 2 + 4 * length * dh * 2 + 4 * blk * length * 4
    return pl.pallas_call(
        functools.partial(_attn_kernel, blk=blk, nblk=nblk, dh=dh),
        grid=(nbatch, n_heads),
        in_specs=[
            pl.BlockSpec((1, 1, length, dh2), lambda b, h: (b, h, 0, 0)),
            pl.BlockSpec((1, 1, length, dh2), lambda b, h: (b, h, 0, 0)),
            pl.BlockSpec((1, 1, vt.shape[2], length), lambda b, h: (b, h, 0, 0)),
        ],
        out_specs=pl.BlockSpec((length, dh), lambda b, h: (b, h)),
        out_shape=jax.ShapeDtypeStruct((nbatch * length, n_heads * dh), BF16),
        scratch_shapes=[pltpu.VMEM((4, length, blk), F32)],
        compiler_params=_params(("parallel", "parallel"), est),
        name="moba_attention",
    )(qaug, kaug, vt)


def _rope_tables(length, dh):
    half = dh // 2
    inv_freq = ROPE_THETA ** (-jnp.arange(half, dtype=F32) * 2.0 / dh)
    ang = jnp.arange(length, dtype=F32)[:, None] * inv_freq[None, :]
    cos, sin = jnp.cos(ang), jnp.sin(ang)
    return jnp.concatenate([cos, cos], axis=-1), jnp.concatenate([-sin, sin], axis=-1)


def kernel(x, a_norm, s5_A_re, s5_A_im, s5_log_dt, s5_B_re, s5_B_im, s5_C_re, s5_C_im, s5_D,
           w_glu, b_glu, kv_norm, w_kv, k_norm, b_norm, w_q, q_norm, w_o, ffn_norm, w_up,
           conv_w, conv_b, w_down):
    nbatch, length, d = x.shape
    n_a = a_norm.shape[0]
    n_b = b_norm.shape[0]
    rows = nbatch * length
    assert nbatch == V7X_SUBLANES, "the S5 recurrence keeps the batch on the sublane axis"
    assert length % MOBA_BLOCK == 0 and n_a >= 1 and n_b >= 1
    dh = k_norm.shape[0]
    cc, ss = _rope_tables(length, dh)

    depth = n_a + n_b
    g = s5_A_re.shape[1]
    w2, w3, w1, lam2 = _s5_prepare(s5_A_re, s5_A_im, s5_log_dt, s5_B_re, s5_B_im, s5_C_re,
                                   s5_C_im, min(g, V7X_LANES // (d // g)))
    a_norm3, d_skip3 = a_norm.reshape(n_a, 1, d), s5_D.reshape(n_a, 1, d)
    wglu, b_glu3 = w_glu.astype(BF16), b_glu.reshape(n_a, 1, 2 * d)
    fnorm3 = ffn_norm.reshape(depth, 1, d)
    conv_b3 = conv_b.reshape(depth, 1, conv_b.shape[-1])
    wo = w_o.astype(BF16)

    h = x
    tm_a = min(1024, rows)
    for i in range(n_a):
        res = _s5_layer(h, nbatch, length, d, i, i == 0, a_norm3, w2, lam2, w3, w1, d_skip3,
                        wglu, b_glu3, next_f32=(w_up, w_down) if i == 0 else None)
        h, wup, wdown = res if i == 0 else (res, wup, wdown)
        h, wup, wdown = _ffn_layer(h.reshape(rows, d), rows, d, nbatch, rows // tm_a, tm_a, i,
                                   fnorm3, wup, conv_w, conv_b3, wdown,
                                   batch_major_out=(i == n_a - 1), next_f32=(w_up, w_down))
        if i < n_a - 1:
            h = h.reshape(length // 2, 2, nbatch, d)

    h = h.reshape(rows, d)
    kaug, vt, kmean = _kv_proj(h, nbatch, length, d, kv_norm, w_kv, k_norm, cc, ss)
    kmean = kmean.transpose(0, 2, 1, 3)
    tm_b = min(512, length)
    for j in range(n_b):
        qaug = _q_proj(h, nbatch, length, d, b_norm[j], w_q[j], q_norm[j], cc, ss, kmean)
        o = _attention(qaug, kaug, vt)
        last = j == n_b - 1
        res = _ffn_layer(h, rows, d, 1, length // tm_b, tm_b, n_a + j, fnorm3, wup, conv_w,
                         conv_b3, wdown, attn=(o, wo, j),
                         next_f32=None if last else (w_up, w_down))
        h, wup, wdown = (res, None, None) if last else res
    return h.reshape(nbatch, length, d)
```

```python
import functools

import jax
import jax.numpy as jnp
from jax import lax
from jax.experimental import pallas as pl
from jax.experimental.pallas import tpu as pltpu

MOBA_BLOCK = 256
MOBA_TOPK = 3
CONV_WIDTH = 3
ROPE_THETA = 10000.0
EPS = 1e-6

V7X_SUBLANES = 8
V7X_LANES = 128
V7X_MXU_DIM = 256
V7X_VMEM_BYTES = 64 * 1024 * 1024

MASK_BIAS = -1e30
LOG2_E = 1.4426950408889634
V_EXTRA_ROWS = 16

F32 = jnp.float32
BF16 = jnp.bfloat16


def _vmem_limit(est_bytes):
    return int(min(V7X_VMEM_BYTES - 6 * 1024 * 1024, max(32 * 1024 * 1024, 2 * est_bytes)))


def _params(semantics, est_bytes):
    return pltpu.CompilerParams(dimension_semantics=semantics,
                                vmem_limit_bytes=_vmem_limit(est_bytes))


def _resident(shape):
    nd = len(shape)
    return pl.BlockSpec(shape, lambda *_: (0,) * nd, pipeline_mode=pl.Buffered(1))


def _layer_resident(arr, layer):
    nd = arr.ndim - 1
    return pl.BlockSpec((None,) + arr.shape[1:], lambda *_: (layer,) + (0,) * nd,
                        pipeline_mode=pl.Buffered(1))


def _cast_rows(total, n_steps):
    tile = 2 * V7X_SUBLANES
    return next(r for r in range(tile, total + 1, tile) if total % r == 0 and total // r <= n_steps)


def _rmsnorm(x, g):
    return x * lax.rsqrt(jnp.mean(x * x, axis=-1, keepdims=True) + EPS) * g


def _s5_kernel(*refs, n_slabs, slab_ch, nstate, nbatch, npairs, batch_major_in, cast_next):
    n_in = n_slabs * slab_ch // V7X_LANES if batch_major_in else 1
    h_refs = refs[:n_in]
    (anorm_ref, w2_ref, lam2_ref, w3_ref, w1_ref, dskip_ref, wglu_ref, bglu_ref,
     *refs) = refs[n_in:]
    if cast_next:
        wup_f32_ref, wdown_f32_ref, out_ref, wup_next_ref, wdown_next_ref, *refs = refs
        wup_next_ref[...] = wup_f32_ref[...].astype(BF16)
        wdown_next_ref[...] = wdown_f32_ref[...].astype(BF16)
    else:
        out_ref, *refs = refs
    state_ref, z_ref, x_ref, xn_ref, pair_ref, g_ref, *xt = refs
    half = npairs * nbatch
    d = x_ref.shape[-1]

    @pl.when(pl.program_id(0) == 0)
    def _():
        state_ref[...] = jnp.zeros_like(state_ref)

    if batch_major_in:
        xt_ref = xt[0]
        for s in range(n_in):
            for b in range(nbatch):
                for par in range(2):
                    xt_ref[s, pl.ds(par * half + b, npairs, stride=nbatch), :] = (
                        h_refs[s][b, pl.ds(par, npairs, stride=2), :])
            x_ref[:, s * V7X_LANES:(s + 1) * V7X_LANES] = xt_ref[s]
    else:
        h_ref = h_refs[0]
        x_ref[pl.ds(0, half), :] = h_ref[:, 0].reshape(half, d)
        x_ref[pl.ds(half, half), :] = h_ref[:, 1].reshape(half, d)

    xn = _rmsnorm(x_ref[...], anorm_ref[...])
    xn_ref[...] = xn
    xnb = xn.astype(BF16)

    for s in range(n_slabs):
        cs = slice(s * slab_ch, (s + 1) * slab_ch)
        pair_ref[s] = jnp.concatenate([xnb[:half, cs], xnb[half:, cs]], axis=1)
        zb = z_ref.at[s % 2]
        zb[...] = jnp.dot(pair_ref[s], w2_ref[s], preferred_element_type=F32)
        lr = jnp.broadcast_to(lam2_ref[s, 0:1, :], (nbatch, nstate))
        li = jnp.broadcast_to(lam2_ref[s, 1:2, :], (nbatch, nstate))
        sr = state_ref[s, :, pl.ds(0, nstate)]
        si = state_ref[s, :, pl.ds(nstate, nstate)]
        for k in range(npairs):
            rows = pl.ds(k * nbatch, nbatch)
            zr = zb[rows, pl.ds(0, nstate)]
            zi = zb[rows, pl.ds(nstate, nstate)]
            zb[rows, pl.ds(0, nstate)] = sr
            zb[rows, pl.ds(nstate, nstate)] = si
            sr, si = lr * sr - li * si + zr, lr * si + li * sr + zi
        state_ref[s, :, pl.ds(0, nstate)] = sr
        state_ref[s, :, pl.ds(nstate, nstate)] = si

        y = (jnp.dot(zb[...].astype(BF16), w3_ref[s], preferred_element_type=F32)
             + jnp.dot(pair_ref[s], w1_ref[s], preferred_element_type=F32))
        dsk = dskip_ref[:, cs]
        g_ref[pl.ds(0, half), cs] = jax.nn.gelu(
            y[:, :slab_ch] + dsk * xn_ref[pl.ds(0, half), cs]).astype(BF16)
        g_ref[pl.ds(half, half), cs] = jax.nn.gelu(
            y[:, slab_ch:] + dsk * xn_ref[pl.ds(half, half), cs]).astype(BF16)

    nc = V7X_MXU_DIM
    for c0 in range(0, d, nc):
        ca, cb = slice(c0, c0 + nc), slice(d + c0, d + c0 + nc)
        za = jnp.dot(g_ref[...], wglu_ref[:, ca], preferred_element_type=F32) + bglu_ref[:, ca]
        zb = jnp.dot(g_ref[...], wglu_ref[:, cb], preferred_element_type=F32) + bglu_ref[:, cb]
        res = x_ref[:, ca] + za * jax.nn.sigmoid(zb)
        out_ref[:, 0, :, ca] = res[:half].reshape(npairs, nbatch, nc)
        out_ref[:, 1, :, ca] = res[half:].reshape(npairs, nbatch, nc)


def _block_diag_lanes(src, n_blocks, row_block, col_block):
    rows, k = src.shape[-3], src.shape[-2]
    kin, kout = k * col_block, k * n_blocks * col_block
    ci = lax.broadcasted_iota(jnp.int32, (kin, kout), 0)
    co = lax.broadcasted_iota(jnp.int32, (kin, kout), 1)
    expand = jnp.where((ci // col_block == co // (n_blocks * col_block))
                       & (ci % col_block == co % col_block), 1.0, 0.0)
    wide = jnp.einsum('...rk,kn->...rn', src.reshape(src.shape[:-2] + (kin,)), expand,
                      precision=lax.Precision.HIGHEST, preferred_element_type=F32).astype(BF16)
    g = (lax.broadcasted_iota(jnp.int32, (rows, kout), 0) // row_block) % n_blocks
    h = (lax.broadcasted_iota(jnp.int32, (rows, kout), 1) // col_block) % n_blocks
    return jnp.where(g == h, wide, jnp.zeros((), BF16))


def _s5_prepare(a_re, a_im, log_dt, b_re, b_im, c_re, c_im, gs):
    na, g, p = a_re.shape
    c = b_re.shape[-1]
    ns = g // gs
    hi = lax.Precision.HIGHEST
    lam = lax.complex(a_re, a_im)
    dt = jnp.exp(log_dt)[..., None]
    lam_bar = jnp.exp(lam * dt)
    b_bar = ((lam_bar - 1.0) / lam)[..., None] * lax.complex(b_re, b_im)
    c_mat = lax.complex(c_re, c_im)
    lb = lam_bar[..., None] * b_bar
    cl1 = c_mat * lam_bar[:, :, None, :]
    cl2 = cl1 * lam_bar[:, :, None, :]
    lam2 = lam_bar * lam_bar

    def re_prod(cm, bm):
        return (jnp.einsum('lgdp,lgpc->lgdc', jnp.real(cm), jnp.real(bm), precision=hi)
                - jnp.einsum('lgdp,lgpc->lgdc', jnp.imag(cm), jnp.imag(bm), precision=hi))

    rcb = re_prod(c_mat, b_bar)
    rclb = re_prod(cl1, b_bar)

    s2 = jnp.stack([lb, b_bar], axis=1).reshape(na, 2, ns, gs, p, c)
    s2 = s2.transpose(0, 2, 1, 3, 5, 4)
    s2 = jnp.stack([jnp.real(s2), jnp.imag(s2)], axis=-2)
    w2 = _block_diag_lanes(s2.reshape(na, ns, 2 * gs * c, 2, p), gs, c, p)
    s3 = jnp.stack([cl1, cl2], axis=1).reshape(na, 2, ns, gs, c, p)
    s3 = s3.transpose(0, 2, 3, 5, 1, 4)
    s3 = jnp.stack([jnp.real(s3), -jnp.imag(s3)], axis=2)
    w3 = _block_diag_lanes(s3.reshape(na, ns, 2 * gs * p, 2, c), gs, p, c)
    zero = jnp.zeros_like(rcb)
    s1 = jnp.stack([jnp.stack([rcb, rclb], axis=1), jnp.stack([zero, rcb], axis=1)], axis=1)
    s1 = s1.reshape(na, 2, 2, ns, gs, c, c)
    s1 = s1.transpose(0, 3, 1, 4, 6, 2, 5)
    w1 = _block_diag_lanes(s1.reshape(na, ns, 2 * gs * c, 2, c), gs, c, c)
    lam2_t = jnp.stack([jnp.real(lam2).reshape(na, ns, gs * p),
                        jnp.imag(lam2).reshape(na, ns, gs * p)], axis=2)
    return w2.astype(BF16), w3.astype(BF16), w1.astype(BF16), lam2_t


def _s5_layer(h, nbatch, length, d, layer, batch_major_in, a_norm, w2, lam2, w3, w1, d_skip,
              wglu, b_glu, next_f32=None):
    n_slabs, slab_ch = w2.shape[1], w2.shape[2] // 2
    nstate = w2.shape[3] // 2
    npairs = min(32, length // 2)
    half = npairs * nbatch
    tb = 2 * half
    if batch_major_in:
        n_in = d // V7X_LANES
        h_specs = [pl.BlockSpec((nbatch, 2 * npairs, V7X_LANES), functools.partial(
            lambda i, s: (0, i, s), s=s)) for s in range(n_in)]
        h_args = [h] * n_in
        extra = [pltpu.VMEM((n_in, tb, V7X_LANES), F32)]
    else:
        h_specs = [pl.BlockSpec((npairs, 2, nbatch, d), lambda i: (i, 0, 0, 0))]
        h_args = [h]
        extra = []
    est = (6 * tb * d * 4 + 2 * half * 2 * nstate * 4 + tb * d * 8 + 3 * tb * d * 2
           + n_slabs * slab_ch * 2 * (4 * nstate + 2 * slab_ch) * 2 + 2 * d * d * 2
           + 2 * tb * 2 * d * 4)
    kern = functools.partial(_s5_kernel, n_slabs=n_slabs, slab_ch=slab_ch, nstate=nstate,
                             nbatch=nbatch, npairs=npairs, batch_major_in=batch_major_in,
                             cast_next=next_f32 is not None)
    n_steps = length // (2 * npairs)
    cast_specs, cast_args = [], []
    out_specs = [pl.BlockSpec((npairs, 2, nbatch, d), lambda i: (i, 0, 0, 0))]
    out_shape = [jax.ShapeDtypeStruct((length // 2, 2, nbatch, d), F32)]
    for w in next_f32 or ():
        rb = _cast_rows(w.shape[1], n_steps)
        last = w.shape[1] // rb - 1
        cast_specs.append(pl.BlockSpec(
            (None, rb, w.shape[2]),
            functools.partial(lambda i, last: (layer, jnp.minimum(i, last), 0), last=last)))
        cast_args.append(w)
        out_specs.append(pl.BlockSpec(
            (rb, w.shape[2]),
            functools.partial(lambda i, last: (jnp.minimum(i, last), 0), last=last)))
        out_shape.append(jax.ShapeDtypeStruct(w.shape[1:], BF16))
    outs = pl.pallas_call(
        kern,
        grid=(n_steps,),
        in_specs=h_specs + [
            _layer_resident(a_norm, layer),
            _layer_resident(w2, layer),
            _layer_resident(lam2, layer),
            _layer_resident(w3, layer),
            _layer_resident(w1, layer),
            _layer_resident(d_skip, layer),
            _layer_resident(wglu, layer),
            _layer_resident(b_glu, layer),
        ] + cast_specs,
        out_specs=out_specs,
        out_shape=out_shape,
        scratch_shapes=[
            pltpu.VMEM((n_slabs, nbatch, 2 * nstate), F32),
            pltpu.VMEM((2, half, 2 * nstate), F32),
            pltpu.VMEM((tb, d), F32),
            pltpu.VMEM((tb, d), F32),
            pltpu.VMEM((n_slabs, half, 2 * slab_ch), BF16),
            pltpu.VMEM((tb, d), BF16),
        ] + extra,
        compiler_params=_params(("arbitrary",), est),
        name="s5_mixer",
    )(*h_args, a_norm, w2, lam2, w3, w1, d_skip, wglu, b_glu, *cast_args)
    return outs if next_f32 is not None else outs[0]


def _ffn_kernel(*refs, has_attn, shift, halo, tiles_per_seq, n_fc, fc, tm, n_slots,
                batch_major_out, cast_next):
    refs = list(refs)
    h_ref = refs.pop(0)
    o_ref, wo_ref = (refs.pop(0), refs.pop(0)) if has_attn else (None, None)
    fnorm_ref, wup_ref, cw_ref, cb_ref, wdown_ref = refs[:5]
    refs = refs[5:]
    if cast_next:
        wup_f32_ref, wdown_f32_ref, out_ref, wup_next_ref, wdown_next_ref = refs[:5]
        refs = refs[5:]
        wup_next_ref[...] = wup_f32_ref[...].astype(BF16)
        wdown_next_ref[...] = wdown_f32_ref[...].astype(BF16)
    else:
        out_ref = refs.pop(0)
    carry_ref, xnb_ref, act_ref, *hu = refs
    aligned = shift % V7X_SUBLANES == 0

    @pl.when(pl.program_id(0) % tiles_per_seq == 0)
    def _():
        carry_ref[...] = jnp.zeros_like(carry_ref)

    if batch_major_out:
        *hu, h1_ref, res_ref = hu
    else:
        h1_ref = out_ref
    h1 = h_ref[...]
    if has_attn:
        h1 = h1 + jnp.dot(o_ref[...], wo_ref[...], preferred_element_type=F32)
    h1_ref[...] = h1
    xnb_ref[...] = _rmsnorm(h1, fnorm_ref[...]).astype(BF16)

    def conv_branch(idx, slot):
        cols = slice(idx * fc, (idx + 1) * fc)
        u = jnp.dot(xnb_ref[...], wup_ref[:, cols], preferred_element_type=F32)
        w = cw_ref[:, cols]
        if aligned:
            prev = carry_ref[idx]
            carry_ref[idx] = u[tm - halo:, :]
            u1 = jnp.concatenate([prev[halo - shift:, :], u[:tm - shift, :]], axis=0)
            u2 = jnp.concatenate([prev[halo - 2 * shift:, :], u[:tm - 2 * shift, :]], axis=0)
        else:
            hu_ref = hu[0]
            hu_ref[slot, pl.ds(0, halo), :] = carry_ref[idx]
            hu_ref[slot, pl.ds(halo, tm), :] = u
            carry_ref[idx] = u[tm - halo:, :]
            u1 = hu_ref[slot, pl.ds(halo - shift, tm), :]
            u2 = hu_ref[slot, pl.ds(halo - 2 * shift, tm), :]
        return w[2:3, :] * u + w[1:2, :] * u1 + w[0:1, :] * u2 + cb_ref[:, cols]

    for c in range(n_fc):
        gate = conv_branch(c, (2 * c) % n_slots)
        val = conv_branch(n_fc + c, (2 * c + 1) % n_slots)
        act_ref[:, c * fc:(c + 1) * fc] = (jax.nn.silu(gate) * val).astype(BF16)

    res = h1_ref[...] + jnp.dot(act_ref[...], wdown_ref[...], preferred_element_type=F32)
    if batch_major_out:
        nbatch, tpb, d = out_ref.shape
        for s in range(d // V7X_LANES):
            res_ref[s] = res[:, s * V7X_LANES:(s + 1) * V7X_LANES]
        for b in range(nbatch):
            for s in range(d // V7X_LANES):
                out_ref[b, :, s * V7X_LANES:(s + 1) * V7X_LANES] = (
                    res_ref[s, pl.ds(b, tpb, stride=nbatch), :])
    else:
        out_ref[...] = res


def _ffn_layer(h, n_rows, d, shift, tiles_per_seq, tm, layer, ffn_norm, wup, conv_w,
               conv_b, wdown, attn=None, batch_major_out=False, next_f32=None):
    f = wdown.shape[0]
    fc = V7X_MXU_DIM
    n_fc = f // fc
    n_steps = n_rows // tm
    halo = -(-2 * shift // V7X_SUBLANES) * V7X_SUBLANES
    aligned = shift % V7X_SUBLANES == 0
    n_slots = 4
    row_map = lambda i: (i, 0)
    in_specs = [pl.BlockSpec((tm, d), row_map)]
    args = [h]
    if attn is not None:
        o, wo, j = attn
        in_specs += [pl.BlockSpec((tm, d), row_map), _layer_resident(wo, j)]
        args += [o, wo]
    in_specs += [_layer_resident(ffn_norm, layer), _resident(wup.shape),
                 _layer_resident(conv_w, layer), _layer_resident(conv_b, layer),
                 _resident(wdown.shape)]
    args += [ffn_norm, wup, conv_w, conv_b, wdown]
    scratch = [
        pltpu.VMEM((2 * n_fc, halo, fc), F32),
        pltpu.VMEM((tm, d), BF16),
        pltpu.VMEM((tm, f), BF16),
    ]
    if not aligned:
        scratch.append(pltpu.VMEM((n_slots, tm + halo, fc), F32))
    if batch_major_out:
        scratch += [pltpu.VMEM((tm, d), F32), pltpu.VMEM((d // V7X_LANES, tm, V7X_LANES), F32)]
        out_specs = [pl.BlockSpec((shift, tm // shift, d), lambda i: (0, i, 0))]
        out_shape = [jax.ShapeDtypeStruct((shift, n_rows // shift, d), F32)]
    else:
        out_specs = [pl.BlockSpec((tm, d), row_map)]
        out_shape = [jax.ShapeDtypeStruct((n_rows, d), F32)]
    if next_f32 is not None:
        for w in next_f32:
            rb = _cast_rows(w.shape[1], n_steps)
            last = w.shape[1] // rb - 1
            in_specs.append(pl.BlockSpec(
                (None, rb, w.shape[2]),
                functools.partial(lambda i, last: (layer + 1, jnp.minimum(i, last), 0), last=last)))
            args.append(w)
            out_specs.append(pl.BlockSpec(
                (rb, w.shape[2]),
                functools.partial(lambda i, last: (jnp.minimum(i, last), 0), last=last)))
            out_shape.append(jax.ShapeDtypeStruct(w.shape[1:], BF16))
    est = (d * 2 * f * 2 + f * d * 2 + d * d * 2 + 8 * tm * d * 4 + tm * d * 2 + tm * f * 2
           + n_slots * (tm + halo) * fc * 4 + 2 * n_fc * halo * fc * 4 + 8 * tm * fc * 4)
    kern = functools.partial(_ffn_kernel, has_attn=attn is not None, shift=shift, halo=halo,
                             tiles_per_seq=tiles_per_seq, n_fc=n_fc, fc=fc, tm=tm,
                             n_slots=n_slots, batch_major_out=batch_major_out,
                             cast_next=next_f32 is not None)
    outs = pl.pallas_call(
        kern,
        grid=(n_steps,),
        in_specs=in_specs,
        out_specs=out_specs,
        out_shape=out_shape,
        scratch_shapes=scratch,
        compiler_params=_params(("arbitrary",), est),
        name="conv_ffn_attn" if attn is not None else "conv_ffn",
    )(*args)
    return outs if next_f32 is not None else outs[0]


def _rope(x, cc, ss, dh):
    return x * cc + pltpu.roll(x, dh // 2, axis=1) * ss


def _kv_kernel(h_ref, nrm_ref, wk_ref, wvt_ref, knorm_ref, cc_ref, ss_ref,
               kaug_ref, vt_ref, kmean_ref, *, n_heads, dh):
    xn = _rmsnorm(h_ref[...], nrm_ref[...]).astype(BF16)
    kv = jnp.dot(xn, wk_ref[...], preferred_element_type=F32)
    vt = lax.dot_general(wvt_ref[...], xn, (((1,), (1,)), ((), ())),
                         preferred_element_type=F32)
    rows = kv.shape[0]
    blk = pl.program_id(1)
    lane = lax.broadcasted_iota(jnp.int32, (rows, dh), 1)
    onehot = jnp.where(lane == blk, 1.0, 0.0).astype(BF16)
    ones_row = jnp.where(lax.broadcasted_iota(jnp.int32, (V_EXTRA_ROWS, rows), 0) == 0,
                         1.0, 0.0).astype(BF16)
    cc = cc_ref[...]
    ss = ss_ref[...]
    means = []
    for hd in range(n_heads):
        kh = _rmsnorm(kv[:, hd * dh:(hd + 1) * dh], knorm_ref[...])
        kr = _rope(kh, cc, ss, dh)
        kaug_ref[0, hd, :, pl.ds(0, dh)] = kr.astype(BF16)
        kaug_ref[0, hd, :, pl.ds(dh, dh)] = onehot
        vt_ref[0, hd, pl.ds(0, dh), :] = vt[hd * dh:(hd + 1) * dh, :].astype(BF16)
        vt_ref[0, hd, pl.ds(dh, V_EXTRA_ROWS), :] = ones_row
        means.append(jnp.mean(kr, axis=0, keepdims=True))
    kmean_ref[0, 0] = jnp.concatenate(means, axis=0)


def _kv_proj(h, nbatch, length, d, kv_norm, w_kv, k_norm, cc, ss):
    dh = k_norm.shape[0]
    n_heads = w_kv.shape[1] // (2 * dh)
    nblk = length // MOBA_BLOCK
    t = MOBA_BLOCK
    est = w_kv.size * 2 + 2 * t * d * 4 + 3 * t * 2 * n_heads * dh * 4 + 6 * n_heads * t * dh * 2
    return pl.pallas_call(
        functools.partial(_kv_kernel, n_heads=n_heads, dh=dh),
        grid=(nbatch, nblk),
        in_specs=[
            pl.BlockSpec((t, d), lambda b, i: (b * nblk + i, 0)),
            _resident((1, d)),
            _resident((d, n_heads * dh)),
            _resident((n_heads * dh, d)),
            _resident((1, dh)),
            pl.BlockSpec((t, dh), lambda b, i: (i, 0)),
            pl.BlockSpec((t, dh), lambda b, i: (i, 0)),
        ],
        out_specs=[
            pl.BlockSpec((1, n_heads, t, 2 * dh), lambda b, i: (b, 0, i, 0)),
            pl.BlockSpec((1, n_heads, dh + V_EXTRA_ROWS, t), lambda b, i: (b, 0, 0, i)),
            pl.BlockSpec((1, 1, n_heads, dh), lambda b, i: (b, i, 0, 0)),
        ],
        out_shape=[
            jax.ShapeDtypeStruct((nbatch, n_heads, length, 2 * dh), BF16),
            jax.ShapeDtypeStruct((nbatch, n_heads, dh + V_EXTRA_ROWS, length), BF16),
            jax.ShapeDtypeStruct((nbatch, nblk, n_heads, dh), F32),
        ],
        compiler_params=_params(("parallel", "parallel"), est),
        name="kv_proj",
    )(h, kv_norm.reshape(1, d), w_kv[:, :n_heads * dh].astype(BF16),
      w_kv[:, n_heads * dh:].T.astype(BF16), k_norm.reshape(1, dh), cc, ss)


def _q_kernel(h_ref, nrm_ref, wq_ref, qnorm_ref, cc_ref, ss_ref, kmean_ref, qaug_ref,
              *, n_heads, dh, nblk, topk, scale):
    xn = _rmsnorm(h_ref[...], nrm_ref[...]).astype(BF16)
    q = jnp.dot(xn, wq_ref[...], preferred_element_type=F32)
    rows = q.shape[0]
    cur = pl.program_id(1)
    cc = cc_ref[...]
    ss = ss_ref[...]
    blk_id = lax.broadcasted_iota(jnp.int32, (nblk, rows), 0)
    for hd in range(n_heads):
        qh = _rmsnorm(q[:, hd * dh:(hd + 1) * dh], qnorm_ref[...])
        qr = _rope(qh, cc, ss, dh)
        gate_t = lax.dot_general(kmean_ref[0, hd], qr, (((1,), (1,)), ((), ())),
                                 precision=lax.Precision.HIGHEST,
                                 preferred_element_type=F32)
        gate_t = jnp.where(blk_id < cur, gate_t, -jnp.inf)
        sel_rows = []
        for j in range(nblk):
            gj = gate_t[j:j + 1, :]
            beats = jnp.where((gate_t > gj) | ((gate_t == gj) & (blk_id < j)), 1.0, 0.0)
            rank = jnp.sum(beats, axis=0, keepdims=True)
            sel_rows.append(
                jnp.where(((rank < topk) & (j < cur)) | (j == cur), 0.0, MASK_BIAS))
        bias_t = jnp.concatenate(
            sel_rows + [jnp.zeros((dh - nblk, rows), F32)], axis=0)
        qaug_ref[0, hd, :, pl.ds(0, dh)] = (qr * scale).astype(BF16)
        qaug_ref[0, hd, :, pl.ds(dh, dh)] = bias_t.T.astype(BF16)


def _q_proj(h, nbatch, length, d, b_norm, w_q, q_norm, cc, ss, kmean):
    dh = q_norm.shape[0]
    n_heads = w_q.shape[1] // dh
    nblk = length // MOBA_BLOCK
    t = MOBA_BLOCK
    est = w_q.size * 2 + 2 * t * d * 4 + 3 * t * n_heads * dh * 4 + 4 * n_heads * t * dh * 2
    kern = functools.partial(_q_kernel, n_heads=n_heads, dh=dh, nblk=nblk,
                             topk=min(MOBA_TOPK, nblk), scale=dh ** -0.5 * LOG2_E)
    return pl.pallas_call(
        kern,
        grid=(nbatch, nblk),
        in_specs=[
            pl.BlockSpec((t, d), lambda b, i: (b * nblk + i, 0)),
            _resident((1, d)),
            _resident(w_q.shape),
            _resident((1, dh)),
            pl.BlockSpec((t, dh), lambda b, i: (i, 0)),
            pl.BlockSpec((t, dh), lambda b, i: (i, 0)),
            pl.BlockSpec((1, n_heads, nblk, dh), lambda b, i: (b, 0, 0, 0)),
        ],
        out_specs=pl.BlockSpec((1, n_heads, t, 2 * dh), lambda b, i: (b, 0, i, 0)),
        out_shape=jax.ShapeDtypeStruct((nbatch, n_heads, length, 2 * dh), BF16),
        compiler_params=_params(("parallel", "parallel"), est),
        name="q_proj_gate",
    )(h, b_norm.reshape(1, d), w_q.astype(BF16), q_norm.reshape(1, dh), cc, ss, kmean)


def _attn_kernel(q_ref, k_ref, vt_ref, o_ref, s_ref, *, blk, nblk, dh):
    nt = (((1,), (1,)), ((), ()))
    kpos = lax.broadcasted_iota(jnp.int32, (blk, blk), 0)
    qpos = lax.broadcasted_iota(jnp.int32, (blk, blk), 1)
    causal = kpos <= qpos

    def score_step(c, j, m_acc):
        st = lax.dot_general(k_ref[0, 0, j * blk:(j + 1) * blk, :],
                             q_ref[0, 0, c * blk:(c + 1) * blk, :], nt,
                             preferred_element_type=F32)
        if j == c:
            st = jnp.where(causal, st, -jnp.inf)
        s_ref[c % 4, j * blk:(j + 1) * blk, :] = st
        part = jnp.max(st.reshape(blk // V7X_SUBLANES, V7X_SUBLANES, blk), axis=0)
        return part if m_acc is None else jnp.maximum(m_acc, part)

    def value_step(c, j, m, acc):
        pt = jnp.exp2(s_ref[c % 4, j * blk:(j + 1) * blk, :] - m)
        pv = jnp.dot(vt_ref[0, 0, :, j * blk:(j + 1) * blk], pt.astype(BF16),
                     preferred_element_type=F32)
        return pv if acc is None else acc + pv

    def reduce_max(m_acc):
        return jnp.max(m_acc, axis=0, keepdims=True)

    ahead = min(3, nblk - 1)
    m_row = {}
    for c in range(ahead):
        m_acc = None
        for j in range(c + 1):
            m_acc = score_step(c, j, m_acc)
        m_row[c] = reduce_max(m_acc)
    for c in range(nblk):
        acc = m_next = None
        nxt = c + ahead
        for j in range(max(c + 1, nxt + 1 if nxt < nblk else 0)):
            if nxt < nblk and j <= nxt:
                m_next = score_step(nxt, j, m_next)
            if j <= c:
                acc = value_step(c, j, m_row[c], acc)
        o_ref[c * blk:(c + 1) * blk, :] = (acc[:dh] / acc[dh:dh + 1]).T.astype(o_ref.dtype)
        if nxt < nblk:
            m_row[nxt] = reduce_max(m_next)


def _attention(qaug, kaug, vt):
    nbatch, n_heads, length, dh2 = qaug.shape
    dh = dh2 // 2
    blk = MOBA_BLOCK
    nblk = length // blk
    est = 4 * length * dh2 * 2 + 4 * length * dh * 2 + 4 * blk * length * 4
    return pl.pallas_call(
        functools.partial(_attn_kernel, blk=blk, nblk=nblk, dh=dh),
        grid=(nbatch, n_heads),
        in_specs=[
            pl.BlockSpec((1, 1, length, dh2), lambda b, h: (b, h, 0, 0)),
            pl.BlockSpec((1, 1, length, dh2), lambda b, h: (b, h, 0, 0)),
            pl.BlockSpec((1, 1, vt.shape[2], length), lambda b, h: (b, h, 0, 0)),
        ],
        out_specs=pl.BlockSpec((length, dh), lambda b, h: (b, h)),
        out_shape=jax.ShapeDtypeStruct((nbatch * length, n_heads * dh), BF16),
        scratch_shapes=[pltpu.VMEM((4, length, blk), F32)],
        compiler_params=_params(("parallel", "parallel"), est),
        name="moba_attention",
    )(qaug, kaug, vt)


def _rope_tables(length, dh):
    half = dh // 2
    inv_freq = ROPE_THETA ** (-jnp.arange(half, dtype=F32) * 2.0 / dh)
    ang = jnp.arange(length, dtype=F32)[:, None] * inv_freq[None, :]
    cos, sin = jnp.cos(ang), jnp.sin(ang)
    return jnp.concatenate([cos, cos], axis=-1), jnp.concatenate([-sin, sin], axis=-1)


def kernel(x, a_norm, s5_A_re, s5_A_im, s5_log_dt, s5_B_re, s5_B_im, s5_C_re, s5_C_im, s5_D,
           w_glu, b_glu, kv_norm, w_kv, k_norm, b_norm, w_q, q_norm, w_o, ffn_norm, w_up,
           conv_w, conv_b, w_down):
    nbatch, length, d = x.shape
    n_a = a_norm.shape[0]
    n_b = b_norm.shape[0]
    rows = nbatch * length
    assert nbatch == V7X_SUBLANES, "the S5 recurrence keeps the batch on the sublane axis"
    assert length % MOBA_BLOCK == 0 and n_a >= 1 and n_b >= 1
    dh = k_norm.shape[0]
    cc, ss = _rope_tables(length, dh)

    depth = n_a + n_b
    g = s5_A_re.shape[1]
    w2, w3, w1, lam2 = _s5_prepare(s5_A_re, s5_A_im, s5_log_dt, s5_B_re, s5_B_im, s5_C_re,
                                   s5_C_im, min(g, V7X_LANES // (d // g)))
    a_norm3, d_skip3 = a_norm.reshape(n_a, 1, d), s5_D.reshape(n_a, 1, d)
    wglu, b_glu3 = w_glu.astype(BF16), b_glu.reshape(n_a, 1, 2 * d)
    fnorm3 = ffn_norm.reshape(depth, 1, d)
    conv_b3 = conv_b.reshape(depth, 1, conv_b.shape[-1])
    wo = w_o.astype(BF16)

    h = x
    tm_a = min(1024, rows)
    for i in range(n_a):
        res = _s5_layer(h, nbatch, length, d, i, i == 0, a_norm3, w2, lam2, w3, w1, d_skip3,
                        wglu, b_glu3, next_f32=(w_up, w_down) if i == 0 else None)
        h, wup, wdown = res if i == 0 else (res, wup, wdown)
        h, wup, wdown = _ffn_layer(h.reshape(rows, d), rows, d, nbatch, rows // tm_a, tm_a, i,
                                   fnorm3, wup, conv_w, conv_b3, wdown,
                                   batch_major_out=(i == n_a - 1), next_f32=(w_up, w_down))
        if i < n_a - 1:
            h = h.reshape(length // 2, 2, nbatch, d)

    h = h.reshape(rows, d)
    kaug, vt, kmean = _kv_proj(h, nbatch, length, d, kv_norm, w_kv, k_norm, cc, ss)
    kmean = kmean.transpose(0, 2, 1, 3)
    tm_b = min(512, length)
    for j in range(n_b):
        qaug = _q_proj(h, nbatch, length, d, b_norm[j], w_q[j], q_norm[j], cc, ss, kmean)
        o = _attention(qaug, kaug, vt)
        last = j == n_b - 1
        res = _ffn_layer(h, rows, d, 1, length // tm_b, tm_b, n_a + j, fnorm3, wup, conv_w,
                         conv_b3, wdown, attn=(o, wo, j),
                         next_f32=None if last else (w_up, w_down))
        h, wup, wdown = (res, None, None) if last else res
    return h.reshape(nbatch, length, d)
```
